```python
import math
import numpy as np
import jax
import jax.numpy as jnp
from jax import lax

D_MODEL = 1024
BATCH = 8
SEQ = 4096
DEPTH = 4

GRID_W = 64
CTX_LEN = 256
EPS = 1e-6

SSD_HEAD_DIM = 64
SSD_INNER = D_MODEL
SSD_HEADS = SSD_INNER // SSD_HEAD_DIM
SSD_GROUPS = 4
SSD_STATE = 64
SSD_CONV = 5
SSD_CHUNK = 128
SSD_CONV_DIM = SSD_INNER + 2 * SSD_GROUPS * SSD_STATE

S5_WIDTH = 3 * D_MODEL // 4
S5_GROUP = 16
S5_GROUPS = S5_WIDTH // S5_GROUP
S5_STATE = 64

DA_HEAD_DIM = 64
DA_V_DIM = 2 * DA_HEAD_DIM
DA_HEADS = D_MODEL // DA_V_DIM
DA_WIDTH = DA_HEADS * DA_V_DIM
DA_QK = DA_HEADS * 2 * DA_HEAD_DIM
Q_BLOCK = 128
ROPE_BASE = 10000.0

D_FF = 4 * D_MODEL
N_BRANCH = 3
IN_SPLITS = (SSD_INNER, SSD_CONV_DIM, 2 * SSD_HEADS, S5_WIDTH, DA_QK, DA_QK, DA_WIDTH, N_BRANCH * D_MODEL)
IN_COLS = sum(IN_SPLITS)

F32 = jnp.float32

kernel_name = 'hybrid_gated_ssd_s5_diffattn_prefix'


def _rmsnorm(x, w):
    xf = x.astype(F32)
    y = xf * lax.rsqrt(jnp.mean(xf * xf, axis=-1, keepdims=True) + EPS)
    return (y * w.astype(F32)).astype(x.dtype)


def _modulate(h, shift, scale):
    return h * (1 + scale) + shift


def _split_cols(p):
    return jnp.split(p, np.cumsum(IN_SPLITS)[:-1].tolist(), axis=-1)


def _rev_parts(u, n_ctx):
    return jnp.concatenate([jnp.flip(u[:, :n_ctx], 1), jnp.flip(u[:, n_ctx:], 1)], axis=1)


def _dwconv(u, w, b):
    k, ch = w.shape
    y = lax.conv_general_dilated(u, w[:, None, :].astype(u.dtype), window_strides=(1,), padding=[(k // 2, k // 2)], dimension_numbers=('NWC', 'WIO', 'NWC'), feature_group_count=ch)
    return y + b.astype(u.dtype)


def _ssd_chunked(x, dt, a, bm, cm):
    bsz, t, nh, hp = x.shape
    ng, ns = bm.shape[2], bm.shape[3]
    r = nh // ng
    q = SSD_CHUNK
    nc = t // q
    xs = (x * dt[..., None]).reshape(bsz, nc, q, ng, r, hp)
    da = (dt * a).reshape(bsz, nc, q, ng, r)
    bs = bm.reshape(bsz, nc, q, ng, ns)
    cs = cm.reshape(bsz, nc, q, ng, ns)
    acum = jnp.cumsum(da, axis=2)
    seg = acum[:, :, :, None] - acum[:, :, None, :]
    lower = jnp.tril(jnp.ones((q, q), dtype=bool))[None, None, :, :, None, None]
    decay = jnp.exp(jnp.where(lower, seg, -jnp.inf))
    cb = jnp.einsum('bclgn,bcsgn->bclsg', cs, bs)
    y_diag = jnp.einsum('bclsgr,bcsgrp->bclgrp', cb[..., None] * decay, xs)
    to_end = jnp.exp(acum[:, :, -1:] - acum)
    states = jnp.einsum('bclgn,bclgrp->bcgrpn', bs, xs * to_end[..., None])
    chunk_decay = jnp.exp(acum[:, :, -1])

    def step(h, inp):
        s, dcy = inp
        return h * dcy[..., None, None] + s, h

    h0 = jnp.zeros((bsz, ng, r, hp, ns), x.dtype)
    _, h_in = lax.scan(step, h0, (jnp.moveaxis(states, 1, 0), jnp.moveaxis(chunk_decay, 1, 0)))
    h_in = jnp.moveaxis(h_in, 0, 1)
    y_off = jnp.einsum('bclgn,bcgrpn->bclgrp', cs, h_in) * jnp.exp(acum)[..., None]
    return (y_diag + y_off).reshape(bsz, t, nh, hp)


def _ssd_mixer(z, xbc, dt_raw, conv_w, conv_b, dt_bias, a_log, d_skip, norm_w, n_ctx, keep_ctx):
    dtype = z.dtype
    xbc = jnp.concatenate([_dwconv(xbc[:, :n_ctx], conv_w, conv_b), _dwconv(xbc[:, n_ctx:], conv_w, conv_b)], axis=1)
    xbc = jax.nn.silu(xbc.astype(F32))
    bsz, t, _ = xbc.shape
    gn = SSD_GROUPS * SSD_STATE
    xh = xbc[..., :SSD_INNER].reshape(bsz, t, SSD_HEADS, SSD_HEAD_DIM)
    bm = xbc[..., SSD_INNER:SSD_INNER + gn].reshape(bsz, t, SSD_GROUPS, SSD_STATE)
    cm = xbc[..., SSD_INNER + gn:].reshape(bsz, t, SSD_GROUPS, SSD_STATE)
    dt_raw = dt_raw.astype(F32)
    y = xh * d_skip.astype(F32)[:, None]

    def rv(u):
        return _rev_parts(u, n_ctx)

    for d in range(2):
        dt = jax.nn.softplus(dt_raw[..., d * SSD_HEADS:(d + 1) * SSD_HEADS] + dt_bias[d].astype(F32))
        a = -jnp.exp(a_log[d].astype(F32))
        if d == 0:
            y = y + _ssd_chunked(xh, dt, a, bm, cm)
        else:
            y = y + rv(_ssd_chunked(rv(xh), rv(dt), a, rv(bm), rv(cm)))
    y = y.reshape(bsz, t, SSD_INNER)
    if not keep_ctx:
        y, z = y[:, n_ctx:], z[:, n_ctx:]
    y = y * jax.nn.silu(z.astype(F32))
    yg = y.reshape(y.shape[0], y.shape[1], SSD_GROUPS, SSD_INNER // SSD_GROUPS)
    yg = yg * lax.rsqrt(jnp.mean(yg * yg, axis=-1, keepdims=True) + EPS)
    return (yg.reshape(y.shape) * norm_w.astype(F32)).astype(dtype)


def _complex_linear_scan(ar, ai, br, bi):
    t = br.shape[1]
    a_r = jnp.broadcast_to(ar[None, None], (1, t) + ar.shape)
    a_i = jnp.broadcast_to(ai[None, None], (1, t) + ai.shape)

    def combine(e1, e2):
        a1r, a1i, b1r, b1i = e1
        a2r, a2i, b2r, b2i = e2
        return (a2r * a1r - a2i * a1i, a2r * a1i + a2i * a1r, a2r * b1r - a2i * b1i + b2r, a2r * b1i + a2i * b1r + b2i)

    _, _, hr, hi = lax.associative_scan(combine, (a_r, a_i, br, bi), axis=1)
    return hr, hi


def _s5_mixer(u, b_re, b_im, c_re, c_im, lam_re, lam_im, log_step, d_skip, w_glu, b_glu, n_ctx, keep_ctx):
    dtype = u.dtype
    uf = u.astype(F32)
    bsz, t, w = uf.shape
    ug = uf.reshape(bsz, t, S5_GROUPS, S5_GROUP)
    bu_r = jnp.einsum('btgk,gpk->btgp', ug, b_re.astype(F32))
    bu_i = jnp.einsum('btgk,gpk->btgp', ug, b_im.astype(F32))
    y = uf * d_skip.astype(F32)
    for d in range(2):
        lr = jnp.minimum(lam_re[d].astype(F32), -1e-4)
        li = lam_im[d].astype(F32)
        step = jnp.exp(log_step[d].astype(F32))[:, None]
        mag = jnp.exp(lr * step)
        ar, ai = mag * jnp.cos(li * step), mag * jnp.sin(li * step)
        den = lr * lr + li * li
        fr = ((ar - 1) * lr + ai * li) / den
        fi = (ai * lr - (ar - 1) * li) / den
        if d == 0:
            ur, ui = bu_r, bu_i
        else:
            ur, ui = _rev_parts(bu_r, n_ctx), _rev_parts(bu_i, n_ctx)
        hr, hi = _complex_linear_scan(ar, ai, ur * fr - ui * fi, ur * fi + ui * fr)
        yd = jnp.einsum('btgp,gkp->btgk', hr, c_re.astype(F32)) - jnp.einsum('btgp,gkp->btgk', hi, c_im.astype(F32))
        if d == 1:
            yd = _rev_parts(yd, n_ctx)
        y = y + yd.reshape(bsz, t, w)
    if not keep_ctx:
        y = y[:, n_ctx:]
    y = jax.nn.gelu(y)
    y = y * jax.nn.sigmoid(y @ w_glu.astype(F32) + b_glu.astype(F32))
    return y.astype(dtype)


def _axial_rope_tables(n_tokens):
    rows = n_tokens // GRID_W
    row = jnp.repeat(jnp.arange(rows, dtype=F32), GRID_W)
    col = jnp.tile(jnp.arange(GRID_W, dtype=F32), rows)
    axis_dim = DA_HEAD_DIM // 2
    inv_freq = ROPE_BASE ** (-jnp.arange(0, axis_dim, 2, dtype=F32) / axis_dim)
    ang = jnp.stack([row[:, None] * inv_freq, col[:, None] * inv_freq], axis=1)
    return jnp.cos(ang), jnp.sin(ang)


def _rope2d(x, cos, sin):
    bsz, n, nh, m, dh = x.shape
    xf = x.astype(F32).reshape(bsz, n, nh, m, 2, dh // 2)
    x1, x2 = jnp.split(xf, 2, axis=-1)
    cs = cos[None, :, None, None]
    sn = sin[None, :, None, None]
    out = jnp.concatenate([x1 * cs - x2 * sn, x2 * cs + x1 * sn], axis=-1)
    return out.reshape(x.shape).astype(x.dtype)


def _diff_attention(q, k, v, lam, subln_w, lam_init, n_ctx, keep_ctx):
    bsz, t = q.shape[:2]
    n_lat = t - n_ctx
    cos, sin = _axial_rope_tables(n_lat)
    q_lat = _rope2d(q[:, n_ctx:], cos, sin)
    k_all = jnp.concatenate([k[:, :n_ctx], _rope2d(k[:, n_ctx:], cos, sin)], axis=1)
    scale = DA_HEAD_DIM ** -0.5

    def attend(qb, kk, vv):
        s = jnp.einsum('bqhmd,bkhmd->bhmqk', qb, kk).astype(F32) * scale
        p = jax.nn.softmax(s, axis=-1)
        wts = p[:, :, 0] - lam * p[:, :, 1]
        return jnp.einsum('bhqk,bkhv->bqhv', wts, vv.astype(F32))

    nb = n_lat // Q_BLOCK
    qb = jnp.moveaxis(q_lat.reshape(bsz, nb, Q_BLOCK, DA_HEADS, 2, DA_HEAD_DIM), 1, 0)
    o = lax.map(lambda blk: attend(blk, k_all, v), qb)
    o = jnp.moveaxis(o, 0, 1).reshape(bsz, n_lat, DA_HEADS, DA_V_DIM)
    if keep_ctx:
        o_ctx = attend(q[:, :n_ctx], k[:, :n_ctx], v[:, :n_ctx])
        o = jnp.concatenate([o_ctx, o], axis=1)
    o = _rmsnorm(o, subln_w) * (1 - lam_init)
    return o.reshape(bsz, o.shape[1], DA_WIDTH).astype(q.dtype)


def _sqrelu_mlp(h, w1, w2):
    a = jax.nn.relu(h @ w1)
    return (a * a) @ w2


def setup_inputs(seed: int = 0) -> dict:
    key = jax.random.key(seed)
    ks = iter(jax.random.split(key, 48))

    def normal(shape, scale):
        return jax.random.normal(next(ks), shape, F32) * scale

    def gain(shape):
        return 1.0 + 0.05 * jax.random.normal(next(ks), shape, F32)

    def log_uniform(shape, lo, hi):
        return jax.random.uniform(next(ks), shape, F32, math.log(lo), math.log(hi))

    x = normal((BATCH, SEQ, D_MODEL), 1.0)
    c = normal((BATCH, D_MODEL), 1.0)
    ctx = normal((BATCH, CTX_LEN, D_MODEL), 1.0)
    c_ctx = normal((D_MODEL,), 1.0)
    w_mod = normal((DEPTH, D_MODEL, 6 * D_MODEL), 0.5 * D_MODEL ** -0.5)
    b_mod = normal((DEPTH, 6 * D_MODEL), 0.02)
    norm1_w = gain((DEPTH, D_MODEL))
    w_in = normal((DEPTH, D_MODEL, IN_COLS), D_MODEL ** -0.5)
    ssd_conv_w = normal((DEPTH, SSD_CONV, SSD_CONV_DIM), SSD_CONV ** -0.5)
    ssd_conv_b = normal((DEPTH, SSD_CONV_DIM), 0.02)
    dt0 = jnp.exp(log_uniform((DEPTH, 2, SSD_HEADS), 1e-3, 1e-1))
    ssd_dt_bias = dt0 + jnp.log(-jnp.expm1(-dt0))
    ssd_a_log = jnp.log(jax.random.uniform(next(ks), (DEPTH, 2, SSD_HEADS), F32, 1.0, 16.0))
    ssd_d = gain((DEPTH, SSD_HEADS))
    ssd_norm_w = gain((DEPTH, SSD_INNER))
    s5_b_re = normal((DEPTH, S5_GROUPS, S5_STATE, S5_GROUP), (2 * S5_GROUP) ** -0.5)
    s5_b_im = normal((DEPTH, S5_GROUPS, S5_STATE, S5_GROUP), (2 * S5_GROUP) ** -0.5)
    s5_c_re = normal((DEPTH, S5_GROUPS, S5_GROUP, S5_STATE), S5_STATE ** -0.5)
    s5_c_im = normal((DEPTH, S5_GROUPS, S5_GROUP, S5_STATE), S5_STATE ** -0.5)
    s5_lam_re = -0.5 + normal((DEPTH, 2, S5_GROUPS, S5_STATE), 0.01)
    s5_lam_im = jnp.pi * jnp.arange(S5_STATE, dtype=F32) + normal((DEPTH, 2, S5_GROUPS, S5_STATE), 0.01)
    s5_log_step = log_uniform((DEPTH, 2, S5_GROUPS), 1e-3, 1e-1)
    s5_d = normal((DEPTH, S5_WIDTH), 1.0)
    s5_w_glu = normal((DEPTH, S5_WIDTH, S5_WIDTH), S5_WIDTH ** -0.5)
    s5_b_glu = normal((DEPTH, S5_WIDTH), 0.02)
    da_lam_q1 = normal((DEPTH, DA_HEAD_DIM), 0.1)
    da_lam_k1 = normal((DEPTH, DA_HEAD_DIM), 0.1)
    da_lam_q2 = normal((DEPTH, DA_HEAD_DIM), 0.1)
    da_lam_k2 = normal((DEPTH, DA_HEAD_DIM), 0.1)
    da_subln_w = gain((DEPTH, DA_V_DIM))
    w_br_a = normal((DEPTH, SSD_INNER, D_MODEL), SSD_INNER ** -0.5)
    w_br_b = normal((DEPTH, S5_WIDTH, D_MODEL), S5_WIDTH ** -0.5)
    w_br_c = normal((DEPTH, DA_WIDTH, D_MODEL), DA_WIDTH ** -0.5)
    w_out = normal((DEPTH, D_MODEL, D_MODEL), D_MODEL ** -0.5)
    norm2_w = gain((DEPTH, D_MODEL))
    w_ff1 = normal((DEPTH, D_MODEL, D_FF), D_MODEL ** -0.5)
    w_ff2 = normal((DEPTH, D_FF, D_MODEL), D_FF ** -0.5)
    final_norm_w = gain((D_MODEL,))
    return {'x': x, 'c': c, 'ctx': ctx, 'c_ctx': c_ctx, 'w_mod': w_mod, 'b_mod': b_mod, 'norm1_w': norm1_w, 'w_in': w_in, 'ssd_conv_w': ssd_conv_w, 'ssd_conv_b': ssd_conv_b, 'ssd_dt_bias': ssd_dt_bias, 'ssd_a_log': ssd_a_log, 'ssd_d': ssd_d, 'ssd_norm_w': ssd_norm_w, 's5_b_re': s5_b_re, 's5_b_im': s5_b_im, 's5_c_re': s5_c_re, 's5_c_im': s5_c_im, 's5_lam_re': s5_lam_re, 's5_lam_im': s5_lam_im, 's5_log_step': s5_log_step, 's5_d': s5_d, 's5_w_glu': s5_w_glu, 's5_b_glu': s5_b_glu, 'da_lam_q1': da_lam_q1, 'da_lam_k1': da_lam_k1, 'da_lam_q2': da_lam_q2, 'da_lam_k2': da_lam_k2, 'da_subln_w': da_subln_w, 'w_br_a': w_br_a, 'w_br_b': w_br_b, 'w_br_c': w_br_c, 'w_out': w_out, 'norm2_w': norm2_w, 'w_ff1': w_ff1, 'w_ff2': w_ff2, 'final_norm_w': final_norm_w}


def reference(x, c, ctx, c_ctx, w_mod, b_mod, norm1_w, w_in, ssd_conv_w, ssd_conv_b, ssd_dt_bias, ssd_a_log, ssd_d, ssd_norm_w, s5_b_re, s5_b_im, s5_c_re, s5_c_im, s5_lam_re, s5_lam_im, s5_log_step, s5_d, s5_w_glu, s5_b_glu, da_lam_q1, da_lam_k1, da_lam_q2, da_lam_k2, da_subln_w, w_br_a, w_br_b, w_br_c, w_out, norm2_w, w_ff1, w_ff2, final_norm_w):
    n_ctx = ctx.shape[1]
    x_lat, x_ctx = x, ctx
    s_lat, s_ctx = jax.nn.silu(c), jax.nn.silu(c_ctx)
    for i in range(DEPTH):
        keep_ctx = i < DEPTH - 1
        m_lat = jnp.split((s_lat @ w_mod[i] + b_mod[i])[:, None, :], 6, axis=-1)
        m_ctx = jnp.split(s_ctx @ w_mod[i] + b_mod[i], 6, axis=-1)
        h = jnp.concatenate([_modulate(_rmsnorm(x_ctx, norm1_w[i]), m_ctx[0], m_ctx[1]), _modulate(_rmsnorm(x_lat, norm1_w[i]), m_lat[0], m_lat[1])], axis=1)
        bsz, t = h.shape[:2]
        z, xbc, dt_raw, u, q, k, v, gate_raw = _split_cols(h @ w_in[i])
        y_a = _ssd_mixer(z, xbc, dt_raw, ssd_conv_w[i], ssd_conv_b[i], ssd_dt_bias[i], ssd_a_log[i], ssd_d[i], ssd_norm_w[i], n_ctx, keep_ctx)
        y_b = _s5_mixer(u, s5_b_re[i], s5_b_im[i], s5_c_re[i], s5_c_im[i], s5_lam_re[i], s5_lam_im[i], s5_log_step[i], s5_d[i], s5_w_glu[i], s5_b_glu[i], n_ctx, keep_ctx)
        lam_init = 0.8 - 0.6 * math.exp(-0.3 * i)
        lam = jnp.exp(jnp.sum(da_lam_q1[i].astype(F32) * da_lam_k1[i].astype(F32))) - jnp.exp(jnp.sum(da_lam_q2[i].astype(F32) * da_lam_k2[i].astype(F32))) + lam_init
        y_c = _diff_attention(q.reshape(bsz, t, DA_HEADS, 2, DA_HEAD_DIM), k.reshape(bsz, t, DA_HEADS, 2, DA_HEAD_DIM), v.reshape(bsz, t, DA_HEADS, DA_V_DIM), lam, da_subln_w[i], lam_init, n_ctx, keep_ctx)
        if not keep_ctx:
            gate_raw = gate_raw[:, n_ctx:]
        g_a, g_b, g_c = jnp.split(jax.nn.sigmoid(gate_raw), N_BRANCH, axis=-1)
        mix = (g_a * (y_a @ w_br_a[i]) + g_b * (y_b @ w_br_b[i]) + g_c * (y_c @ w_br_c[i])) @ w_out[i]
        if keep_ctx:
            x_ctx = x_ctx + m_ctx[2] * mix[:, :n_ctx]
            x_lat = x_lat + m_lat[2] * mix[:, n_ctx:]
            x_ctx = x_ctx + m_ctx[5] * _sqrelu_mlp(_modulate(_rmsnorm(x_ctx, norm2_w[i]), m_ctx[3], m_ctx[4]), w_ff1[i], w_ff2[i])
        else:
            x_lat = x_lat + m_lat[2] * mix
        x_lat = x_lat + m_lat[5] * _sqrelu_mlp(_modulate(_rmsnorm(x_lat, norm2_w[i]), m_lat[3], m_lat[4]), w_ff1[i], w_ff2[i])
    return _rmsnorm(x_lat, final_norm_w)
```

```python
import functools
import math

import jax
import jax.numpy as jnp
from jax import lax
from jax.experimental import pallas as pl
from jax.experimental.pallas import tpu as pltpu

F32 = jnp.float32
BF16 = jnp.bfloat16

D_MODEL = 1024
EPS = 1e-6
GRID_W = 64
SSD_HEAD_DIM = 64
SSD_INNER = D_MODEL
SSD_HEADS = SSD_INNER // SSD_HEAD_DIM
SSD_GROUPS = 4
SSD_STATE = 64
SSD_CONV = 5
SSD_CHUNK = 128
SSD_BC = SSD_GROUPS * SSD_STATE
SSD_CONV_DIM = SSD_INNER + 2 * SSD_BC
S5_WIDTH = 3 * D_MODEL // 4
S5_GROUP = 16
S5_GROUPS = S5_WIDTH // S5_GROUP
S5_STATE = 64
S5_CHUNK = 16
S5_BLK = S5_CHUNK * S5_GROUP
DA_HEAD_DIM = 64
DA_V_DIM = 2 * DA_HEAD_DIM
DA_HEADS = D_MODEL // DA_V_DIM
ROPE_BASE = 10000.0
D_FF = 4 * D_MODEL
N_BRANCH = 3
IN_SPLITS = (SSD_INNER, SSD_CONV_DIM, 2 * SSD_HEADS, S5_WIDTH, D_MODEL, D_MODEL, D_MODEL, N_BRANCH * D_MODEL)

LANES = 128
MIB = 1024 * 1024
LOG2E = 1.4426950408889634

_HI = lax.Precision.HIGHEST


def _cparams(sem, vmem_mib):
    return pltpu.CompilerParams(dimension_semantics=sem, vmem_limit_bytes=int(vmem_mib * MIB))


def _dot(a, b):
    return jnp.dot(a, b, preferred_element_type=F32)


def _dot_nt(a, b):
    return lax.dot_general(a, b, (((1,), (1,)), ((), ())), preferred_element_type=F32)


def _ctx_select(t_idx, tm, n_ctx, ctx_row, lat_row):
    rows = t_idx * tm + lax.broadcasted_iota(jnp.int32, (tm, 1), 0)
    return jnp.where(rows < n_ctx, ctx_row, lat_row)


def _rms_mod(x, nw, shift, scale):
    y = x * lax.rsqrt(jnp.mean(x * x, axis=-1, keepdims=True) + EPS) * nw
    return y * (1.0 + scale) + shift


def _mods_kernel(s_ref, w_ref, b_ref, o_ref):
    s = s_ref[...]
    s = (s * jax.nn.sigmoid(s)).astype(BF16)
    o_ref[0] = _dot(s, w_ref[0]) + b_ref[0]


def _mods(cvec, w_mod, b_mod):
    depth, d, n = w_mod.shape
    rb = cvec.shape[0]
    tn = 1536
    return pl.pallas_call(
        _mods_kernel,
        grid=(depth, n // tn),
        in_specs=[pl.BlockSpec((rb, d), lambda i, j: (0, 0)),
                  pl.BlockSpec((1, d, tn), lambda i, j: (i, 0, j)),
                  pl.BlockSpec((1, 1, tn), lambda i, j: (i, 0, j))],
        out_specs=pl.BlockSpec((1, rb, tn), lambda i, j: (i, 0, j)),
        out_shape=jax.ShapeDtypeStruct((depth, rb, n), F32),
        compiler_params=_cparams(("arbitrary", "arbitrary"), 24),
    )(cvec, w_mod.astype(BF16), b_mod.reshape(depth, 1, n))


def _norm_mod_kernel(x_ref, ml_ref, mc_ref, nw_ref, o_ref, *, tm, n_ctx):
    t = pl.program_id(1)
    shift = _ctx_select(t, tm, n_ctx, mc_ref[0:1, :], ml_ref[0, 0:1, :])
    scale = _ctx_select(t, tm, n_ctx, mc_ref[1:2, :], ml_ref[0, 1:2, :])
    o_ref[0] = _rms_mod(x_ref[0], nw_ref[...], shift, scale).astype(o_ref.dtype)


def _norm_mod(x, m_lat, m_ctx, nw, n_ctx):
    b, t, d = x.shape
    tm = t // 8
    return pl.pallas_call(
        functools.partial(_norm_mod_kernel, tm=tm, n_ctx=n_ctx),
        grid=(b, t // tm),
        in_specs=[pl.BlockSpec((1, tm, d), lambda i, j: (i, j, 0)),
                  pl.BlockSpec((1, 6, d), lambda i, j: (i, 0, 0)),
                  pl.BlockSpec((6, d), lambda i, j: (0, 0)),
                  pl.BlockSpec((1, d), lambda i, j: (0, 0))],
        out_specs=pl.BlockSpec((1, tm, d), lambda i, j: (i, j, 0)),
        out_shape=jax.ShapeDtypeStruct((b, t, d), BF16),
        compiler_params=_cparams(("arbitrary", "arbitrary"), 32),
    )(x, m_lat, m_ctx, nw.reshape(1, d))


def _proj_kernel(x_ref, w_ref, o_ref, *, act):
    acc = _dot(x_ref[0], w_ref[...])
    if act == "sigmoid":
        acc = jax.nn.sigmoid(acc)
    elif act == "relu2":
        acc = jnp.maximum(acc, 0.0)
        acc = acc * acc
    o_ref[0] = acc.astype(o_ref.dtype)


def _proj(x, w, out_dtype, act=None, tn=512):
    b, t, k = x.shape
    n = w.shape[1]
    tn = min(tn, n)
    tm = t // 2
    return pl.pallas_call(
        functools.partial(_proj_kernel, act=act),
        grid=(b, t // tm, n // tn),
        in_specs=[pl.BlockSpec((1, tm, k), lambda i, j, l: (i, j, 0)),
                  pl.BlockSpec((k, tn), lambda i, j, l: (0, l))],
        out_specs=pl.BlockSpec((1, tm, tn), lambda i, j, l: (i, j, l)),
        out_shape=jax.ShapeDtypeStruct((b, t, n), out_dtype),
        compiler_params=_cparams(("arbitrary",) * 3, 48),
    )(x, w)


def _proj_t_kernel(x_ref, wt_ref, o_ref):
    o_ref[0] = _dot_nt(wt_ref[...], x_ref[0]).astype(o_ref.dtype)


def _proj_t(x, wt, out_dtype, tn=256):
    b, t, k = x.shape
    n = wt.shape[0]
    tn = min(tn, n)
    tm = t // 2
    return pl.pallas_call(
        _proj_t_kernel,
        grid=(b, t // tm, n // tn),
        in_specs=[pl.BlockSpec((1, tm, k), lambda i, j, l: (i, j, 0)),
                  pl.BlockSpec((tn, k), lambda i, j, l: (l, 0))],
        out_specs=pl.BlockSpec((1, tn, tm), lambda i, j, l: (i, l, j)),
        out_shape=jax.ShapeDtypeStruct((b, n, t), out_dtype),
        compiler_params=_cparams(("arbitrary",) * 3, 48),
    )(x, wt)


def _proj_rope_kernel(x_ref, w_ref, cos_ref, sin_ref, o_ref, *, tn, n_q_tiles, q_scale):
    acc = _dot(x_ref[0], w_ref[...])
    cos = cos_ref[...]
    sin = sin_ref[...]
    lane = lax.broadcasted_iota(jnp.int32, cos.shape, 1)
    first_half = (lane % 32) < 16
    scale = jnp.where(pl.program_id(2) < n_q_tiles, q_scale, 1.0).astype(F32)
    for c in range(tn // LANES):
        blk = acc[:, c * LANES:(c + 1) * LANES]
        partner = jnp.where(first_half, pltpu.roll(blk, LANES - 16, axis=1), pltpu.roll(blk, 16, axis=1))
        out = (blk * cos + partner * sin) * scale
        o_ref[0, :, c * LANES:(c + 1) * LANES] = out.astype(o_ref.dtype)


def _proj_rope(x, w, cos, sin, q_scale):
    b, t, k = x.shape
    n = w.shape[1]
    tn = 512
    tm = t // 2
    return pl.pallas_call(
        functools.partial(_proj_rope_kernel, tn=tn, n_q_tiles=(n // 2) // tn, q_scale=q_scale),
        grid=(b, t // tm, n // tn),
        in_specs=[pl.BlockSpec((1, tm, k), lambda i, j, l: (i, j, 0)),
                  pl.BlockSpec((k, tn), lambda i, j, l: (0, l)),
                  pl.BlockSpec((tm, LANES), lambda i, j, l: (j, 0)),
                  pl.BlockSpec((tm, LANES), lambda i, j, l: (j, 0))],
        out_specs=pl.BlockSpec((1, tm, tn), lambda i, j, l: (i, j, l)),
        out_shape=jax.ShapeDtypeStruct((b, t, n), BF16),
        compiler_params=_cparams(("arbitrary",) * 3, 48),
    )(x, w, cos, sin)


def _rope_tables(n_ctx, n_lat):
    rows = n_lat // GRID_W
    row = jnp.repeat(jnp.arange(rows, dtype=F32), GRID_W)
    col = jnp.tile(jnp.arange(GRID_W, dtype=F32), rows)
    axis_dim = DA_HEAD_DIM // 2
    inv_freq = ROPE_BASE ** (-jnp.arange(0, axis_dim, 2, dtype=F32) / axis_dim)
    ang = jnp.stack([row[:, None] * inv_freq, col[:, None] * inv_freq], axis=1)
    cos = jnp.cos(ang)
    sin = jnp.sin(ang)
    cos_h = jnp.concatenate([cos, cos], axis=-1).reshape(n_lat, DA_HEAD_DIM)
    sin_h = jnp.concatenate([-sin, sin], axis=-1).reshape(n_lat, DA_HEAD_DIM)
    cos_t = jnp.concatenate([jnp.ones((n_ctx, DA_HEAD_DIM), F32), cos_h], axis=0)
    sin_t = jnp.concatenate([jnp.zeros((n_ctx, DA_HEAD_DIM), F32), sin_h], axis=0)
    return jnp.tile(cos_t, (1, 2)), jnp.tile(sin_t, (1, 2))


def _attn_kernel(q_ref, k_ref, vt_ref, lamp_ref, sw_ref, o_ref, *, n_ctx, tq, lam_init):
    qi = pl.program_id(2)
    lp = lamp_ref[...]
    lam = (jnp.exp(jnp.sum(lp[0:1] * lp[1:2], axis=-1, keepdims=True))
           - jnp.exp(jnp.sum(lp[2:3] * lp[3:4], axis=-1, keepdims=True)) + lam_init)
    q = q_ref[0]
    lane = lax.broadcasted_iota(jnp.int32, q.shape, 1)
    zero = jnp.zeros_like(q)
    q_maps = (jnp.where(lane < DA_HEAD_DIM, q, zero), jnp.where(lane >= DA_HEAD_DIM, q, zero))

    def attend(nk):
        k = k_ref[0, 0:nk, :]
        vt = vt_ref[0, :, 0:nk]
        outs = []
        for qm in q_maps:
            s = _dot_nt(k, qm)
            m = jnp.max(s, axis=0, keepdims=True)
            p = jnp.exp2(s - m)
            l = jnp.sum(p, axis=0, keepdims=True)
            outs.append(_dot(vt, p.astype(BF16)) * (1.0 / l))
        o = outs[0] - lam * outs[1]
        o = o * lax.rsqrt(jnp.mean(o * o, axis=0, keepdims=True) + EPS)
        o = o * (sw_ref[...] * (1.0 - lam_init))
        o_ref[0] = o.T.astype(o_ref.dtype)

    n_ctx_tiles = n_ctx // tq

    @pl.when(qi < n_ctx_tiles)
    def _():
        attend(n_ctx)

    @pl.when(qi >= n_ctx_tiles)
    def _():
        attend(k_ref.shape[1])


def _attention(qk, vt, lam_params, subln_w, n_ctx, lam_init):
    b, t, _ = qk.shape
    tq = 256 if n_ctx % 256 == 0 else 128
    return pl.pallas_call(
        functools.partial(_attn_kernel, n_ctx=n_ctx, tq=tq, lam_init=lam_init),
        grid=(b, DA_HEADS, t // tq),
        in_specs=[pl.BlockSpec((1, tq, LANES), lambda i, h, j: (i, j, h)),
                  pl.BlockSpec((1, t, LANES), lambda i, h, j: (i, 0, DA_HEADS + h)),
                  pl.BlockSpec((1, LANES, t), lambda i, h, j: (i, h, 0)),
                  pl.BlockSpec((4, DA_HEAD_DIM), lambda i, h, j: (0, 0)),
                  pl.BlockSpec((DA_V_DIM, 1), lambda i, h, j: (0, 0))],
        out_specs=pl.BlockSpec((1, tq, LANES), lambda i, h, j: (i, j, h)),
        out_shape=jax.ShapeDtypeStruct((b, t, D_MODEL), BF16),
        compiler_params=_cparams(("arbitrary",) * 3, 56),
    )(qk, qk, vt, lam_params, subln_w.reshape(DA_V_DIM, 1))


def _conv_silu_kernel(x_ref, w_ref, b_ref, o_ref, *, n_ctx):
    x = x_ref[0]
    t = x.shape[0]
    rows = lax.broadcasted_iota(jnp.int32, (t, 1), 0)
    is_ctx = rows < n_ctx
    lo = jnp.where(is_ctx, 0, n_ctx)
    hi = jnp.where(is_ctx, n_ctx, t)
    acc = x * w_ref[SSD_CONV // 2:SSD_CONV // 2 + 1, :] + b_ref[...]
    for j in range(-(SSD_CONV // 2), SSD_CONV // 2 + 1):
        if j == 0:
            continue
        src = rows + j
        shifted = pltpu.roll(x, (-j) % t, axis=0)
        valid = (src >= lo) & (src < hi)
        acc = acc + jnp.where(valid, shifted, 0.0) * w_ref[j + SSD_CONV // 2:j + SSD_CONV // 2 + 1, :]
    o_ref[0] = acc * jax.nn.sigmoid(acc)


def _conv_silu(xbc, conv_w, conv_b, n_ctx):
    b, t, c = xbc.shape
    tc = 256
    return pl.pallas_call(
        functools.partial(_conv_silu_kernel, n_ctx=n_ctx),
        grid=(b, c // tc),
        in_specs=[pl.BlockSpec((1, t, tc), lambda i, j: (i, 0, j)),
                  pl.BlockSpec((SSD_CONV, tc), lambda i, j: (0, j)),
                  pl.BlockSpec((1, tc), lambda i, j: (0, j))],
        out_specs=pl.BlockSpec((1, t, tc), lambda i, j: (i, 0, j)),
        out_shape=jax.ShapeDtypeStruct((b, t, c), F32),
        compiler_params=_cparams(("arbitrary", "arbitrary"), 56),
    )(xbc, conv_w, conv_b.reshape(1, c))


def _softplus(x):
    return jnp.maximum(x, 0.0) + jnp.log1p(jnp.exp(-jnp.abs(x)))


def _expand(v, e_ref, pieces):
    e = e_ref[...]
    out = None
    rem = v
    for _ in range(pieces):
        part = rem.astype(BF16)
        term = _dot(part, e)
        out = term if out is None else out + term
        rem = rem - part.astype(F32)
    return out


def _ssd_kernel(x_ref, dt_ref, dtt_ref, pr_ref, pc_ref, e64_ref, e128_ref, o_ref, h_ref, *, reverse, d):
    q = SSD_CHUNK
    nh = SSD_HEADS

    @pl.when(pl.program_id(1) == 0)
    def _():
        h_ref[...] = jnp.zeros_like(h_ref)

    xa = x_ref[0]
    xh = xa[:, :SSD_INNER]
    bm = xa[:, SSD_INNER:SSD_INNER + SSD_BC]
    cm = xa[:, SSD_INNER + SSD_BC:]
    bias_r = pr_ref[0:1, :]
    a_r = -jnp.exp(pr_ref[1:2, :])
    bias_c = pc_ref[:, 0:1]
    a_c = -jnp.exp(pc_ref[:, 1:2])
    dt = _softplus(dt_ref[0][:, d * nh:(d + 1) * nh] + bias_r)
    dtt = _softplus(dtt_ref[0][d * nh:(d + 1) * nh, :] + bias_c)
    da = dt * a_r
    dat = dtt * a_c
    ri = lax.broadcasted_iota(jnp.int32, (q, q), 0)
    ci = lax.broadcasted_iota(jnp.int32, (q, q), 1)
    keep = (ci >= ri) if reverse else (ci <= ri)
    tri = keep.astype(F32)
    acum = jnp.dot(tri, da, precision=_HI, preferred_element_type=F32)
    acumt = lax.dot_general(dat, tri, (((1,), (1,)), ((), ())), precision=_HI,
                            preferred_element_type=F32)
    total = acum[0:1, :] if reverse else acum[q - 1:q, :]
    to_end = jnp.exp(total - acum)
    eacum = jnp.exp(acum)
    cdecay = jnp.broadcast_to(jnp.exp(total), (8, nh))
    dt_x = _expand(dt, e64_ref, 2)
    te_x = _expand(to_end, e64_ref, 2)
    ea_x = _expand(eacum, e64_ref, 3)
    cd_x = _expand(cdecay, e64_ref, 3)[0:1, :]
    ac_x = _expand(acum, e128_ref, 3)

    xs = xh * dt_x
    xs_b = xs.astype(BF16)
    xe_b = (xs * te_x).astype(BF16)
    lane = lax.broadcasted_iota(jnp.int32, (q, LANES), 1)
    lo_half = lane < SSD_STATE
    zero_b = jnp.zeros((q, LANES), BF16)

    y_cols = []
    for g in range(SSD_GROUPS):
        pair = g // 2
        bm_p = bm[:, pair * LANES:(pair + 1) * LANES]
        cm_p = cm[:, pair * LANES:(pair + 1) * LANES]
        mine = lo_half if g % 2 == 0 else jnp.logical_not(lo_half)
        cz = jnp.where(mine, cm_p, 0.0).astype(BF16)
        bt = bm_p.T.astype(BF16)
        cb = _dot(cz, bt)
        cols = slice(g * 4 * SSD_HEAD_DIM, (g + 1) * 4 * SSD_HEAD_DIM)
        hp = h_ref[g]
        y_off = _dot(cz, hp.astype(BF16)) * ea_x[:, cols]
        h_ref[g] = hp * cd_x[:, cols] + _dot(bt, xe_b[:, cols])
        y_parts = []
        for hpair in range(2):
            ws = []
            xin = []
            for sub in range(2):
                hh = g * 4 + hpair * 2 + sub
                seg = ac_x[:, hh * LANES:(hh + 1) * LANES] - acumt[hh:hh + 1, :]
                dec = jnp.exp(jnp.where(keep, seg, -jnp.inf))
                ws.append((cb * dec).astype(BF16))
                xp = xs_b[:, (hh // 2) * LANES:(hh // 2 + 1) * LANES]
                xin.append(jnp.where(lo_half if sub == 0 else jnp.logical_not(lo_half), xp, zero_b))
            y_parts.append(_dot(jnp.concatenate(ws, axis=1), jnp.concatenate(xin, axis=0)))
        y_cols.append(jnp.concatenate(y_parts, axis=1) + y_off)
    o_ref[0] = jnp.concatenate(y_cols, axis=1)


def _ssd_direction(xact, dt_raw, dt_raw_t, dt_bias, a_log, e64, e128, n_ctx, reverse):
    b, t, c = xact.shape
    q = SSD_CHUNK
    nc = t // q
    ncc = n_ctx // q
    d = 1 if reverse else 0
    if reverse:
        def cidx(j):
            return jnp.where(j < ncc, ncc - 1 - j, nc - 1 - (j - ncc))
    else:
        def cidx(j):
            return j
    pr = jnp.stack([dt_bias[d], a_log[d]], axis=0).astype(F32)
    return pl.pallas_call(
        functools.partial(_ssd_kernel, reverse=reverse, d=d),
        grid=(b, nc),
        in_specs=[pl.BlockSpec((1, q, c), lambda i, j: (i, cidx(j), 0)),
                  pl.BlockSpec((1, q, 2 * SSD_HEADS), lambda i, j: (i, cidx(j), 0)),
                  pl.BlockSpec((1, 2 * SSD_HEADS, q), lambda i, j: (i, 0, cidx(j))),
                  pl.BlockSpec((2, SSD_HEADS), lambda i, j: (0, 0)),
                  pl.BlockSpec((SSD_HEADS, 2), lambda i, j: (0, 0)),
                  pl.BlockSpec(e64.shape, lambda i, j: (0, 0)),
                  pl.BlockSpec(e128.shape, lambda i, j: (0, 0))],
        out_specs=pl.BlockSpec((1, q, SSD_INNER), lambda i, j: (i, cidx(j), 0)),
        out_shape=jax.ShapeDtypeStruct((b, t, SSD_INNER), F32),
        scratch_shapes=[pltpu.VMEM((SSD_GROUPS, 2 * SSD_STATE, 4 * SSD_HEAD_DIM), F32)],
        compiler_params=_cparams(("arbitrary", "arbitrary"), 32),
    )(xact, dt_raw, dt_raw_t, pr, pr.T, e64, e128)


def _ssd_final_kernel(x_ref, yf_ref, yb_ref, z_ref, d_ref, nw_ref, o_ref):
    y = x_ref[0] * d_ref[...] + yf_ref[0] + yb_ref[0]
    z = z_ref[0].astype(F32)
    y = y * (z * jax.nn.sigmoid(z))
    gw = SSD_INNER // SSD_GROUPS
    for g in range(SSD_GROUPS):
        yg = y[:, g * gw:(g + 1) * gw]
        yg = yg * lax.rsqrt(jnp.mean(yg * yg, axis=-1, keepdims=True) + EPS)
        o_ref[0, :, g * gw:(g + 1) * gw] = (yg * nw_ref[:, g * gw:(g + 1) * gw]).astype(o_ref.dtype)


def _ssd_final(xact, yf, yb, z, d_skip, norm_w):
    b, t, _ = yf.shape
    tm = t // 8
    n = SSD_INNER
    row = pl.BlockSpec((1, tm, n), lambda i, j: (i, j, 0))
    vec = pl.BlockSpec((1, n), lambda i, j: (0, 0))
    return pl.pallas_call(
        _ssd_final_kernel,
        grid=(b, t // tm),
        in_specs=[row, row, row, row, vec, vec],
        out_specs=row,
        out_shape=jax.ShapeDtypeStruct((b, t, n), BF16),
        compiler_params=_cparams(("arbitrary", "arbitrary"), 48),
    )(xact, yf, yb, z, jnp.repeat(d_skip.astype(F32), SSD_HEAD_DIM).reshape(1, n), norm_w.reshape(1, n))


def _s5_prep_kernel(bt_ref, cw_ref, vx_ref, btt_ref, gx_ref, ctt_ref, kall_ref, s_ref, g_ref):
    for d in range(2):
        kall_ref[0, d] = jnp.dot(bt_ref[0], cw_ref[0, d], precision=_HI, preferred_element_type=F32)
    bre = btt_ref[0, 0]
    bim = btt_ref[0, 1]
    cre = ctt_ref[0, 0]
    cim = ctt_ref[0, 1]
    p = S5_STATE
    for d in range(2):
        vr = vx_ref[0, d, 0]
        vi = vx_ref[0, d, 1]
        sr = (vr * bre - vi * bim).astype(BF16)
        si = (vr * bim + vi * bre).astype(BF16)
        s_ref[0, :, (4 * d + 0) * p:(4 * d + 1) * p] = sr
        s_ref[0, :, (4 * d + 1) * p:(4 * d + 2) * p] = si
        s_ref[0, :, (4 * d + 2) * p:(4 * d + 3) * p] = si
        s_ref[0, :, (4 * d + 3) * p:(4 * d + 4) * p] = sr
        gr = gx_ref[0, d, 0]
        gi = gx_ref[0, d, 1]
        g_ref[0, (2 * d) * p:(2 * d + 1) * p, :] = (cre * gr - cim * gi).astype(BF16)
        g_ref[0, (2 * d + 1) * p:(2 * d + 2) * p, :] = (-(cre * gi + cim * gr)).astype(BF16)


def _s5_matrices(b_re, b_im, c_re, c_im, lam_re, lam_im, log_step):
    g, p, kk = b_re.shape
    lc = S5_CHUNK
    b_re, b_im, c_re, c_im = (a.astype(F32) for a in (b_re, b_im, c_re, c_im))
    lr = jnp.minimum(lam_re.astype(F32), -1e-4)
    li = lam_im.astype(F32)
    step = jnp.exp(log_step.astype(F32))[..., None]
    mag = jnp.exp(lr * step)
    ar, ai = mag * jnp.cos(li * step), mag * jnp.sin(li * step)
    den = lr * lr + li * li
    fr = ((ar - 1) * lr + ai * li) / den
    fi = (ai * lr - (ar - 1) * li) / den
    jj = jnp.arange(lc + 1, dtype=F32)[:, None, None, None]
    pmag = jnp.exp(jj * (lr * step)[None])
    pr = pmag * jnp.cos(jj * (li * step)[None])
    pi = pmag * jnp.sin(jj * (li * step)[None])
    wr = pr * fr[None] - pi * fi[None]
    wi = pr * fi[None] + pi * fr[None]

    def rep_cols(a):
        return jnp.repeat(jnp.transpose(a, (1, 2, 0)), kk, axis=-1)

    def rep_rows(a):
        return jnp.repeat(jnp.transpose(a, (1, 0, 2)), kk, axis=1)

    ctt_re = jnp.tile(jnp.transpose(c_re, (0, 2, 1)), (1, 1, lc))
    ctt_im = jnp.tile(jnp.transpose(c_im, (0, 2, 1)), (1, 1, lc))
    btt_re = jnp.tile(jnp.transpose(b_re, (0, 2, 1)), (1, lc, 1))
    btt_im = jnp.tile(jnp.transpose(b_im, (0, 2, 1)), (1, lc, 1))
    bt = jnp.concatenate([jnp.transpose(b_re, (0, 2, 1)), jnp.transpose(b_im, (0, 2, 1))], axis=-1)

    cw, vx, gx = [], [], []
    for d in range(2):
        lag = slice(0, lc)
        wjr, wji = wr[lag, d], wi[lag, d]
        wxr, wxi = rep_cols(wjr), rep_cols(wji)
        cwr = ctt_re * wxr - ctt_im * wxi
        cwi = ctt_re * wxi + ctt_im * wxr
        cw.append(jnp.concatenate([cwr, -cwi], axis=1))
        vr_s, vi_s = (wr[lag, d][::-1], wi[lag, d][::-1]) if d == 0 else (wr[lag, d], wi[lag, d])
        vx.append(jnp.stack([rep_rows(vr_s), rep_rows(vi_s)], axis=1))
        gr_t, gi_t = (pr[1:, d], pi[1:, d]) if d == 0 else (pr[1:, d][::-1], pi[1:, d][::-1])
        gx.append(jnp.stack([rep_cols(gr_t), rep_cols(gi_t)], axis=1))
    cw = jnp.stack(cw, axis=1)
    vx = jnp.stack(vx, axis=1)
    gx = jnp.stack(gx, axis=1)
    btt = jnp.stack([btt_re, btt_im], axis=1)
    ctt = jnp.stack([ctt_re, ctt_im], axis=1)

    blk = S5_BLK
    kall, smat, gcat = pl.pallas_call(
        _s5_prep_kernel,
        grid=(g,),
        in_specs=[pl.BlockSpec((1, kk, 2 * p), lambda i: (i, 0, 0)),
                  pl.BlockSpec((1, 2, 2 * p, blk), lambda i: (i, 0, 0, 0)),
                  pl.BlockSpec((1, 2, 2, blk, p), lambda i: (i, 0, 0, 0, 0)),
                  pl.BlockSpec((1, 2, blk, p), lambda i: (i, 0, 0, 0)),
                  pl.BlockSpec((1, 2, 2, p, blk), lambda i: (i, 0, 0, 0, 0)),
                  pl.BlockSpec((1, 2, p, blk), lambda i: (i, 0, 0, 0))],
        out_specs=[pl.BlockSpec((1, 2, kk, blk), lambda i: (i, 0, 0, 0)),
                   pl.BlockSpec((1, blk, 8 * p), lambda i: (i, 0, 0)),
                   pl.BlockSpec((1, 4 * p, blk), lambda i: (i, 0, 0))],
        out_shape=[jax.ShapeDtypeStruct((g, 2, kk, blk), F32),
                   jax.ShapeDtypeStruct((g, blk, 8 * p), BF16),
                   jax.ShapeDtypeStruct((g, 4 * p, blk), BF16)],
        compiler_params=_cparams(("arbitrary",), 24),
    )(bt, cw, vx, btt, gx, ctt)

    kf = kall[:, 0].reshape(g, kk, lc, kk)
    kb = kall[:, 1].reshape(g, kk, lc, kk)
    s_i = jnp.arange(lc)[:, None]
    t_i = jnp.arange(lc)[None, :]
    tf = jnp.where((t_i >= s_i)[None, None, :, :, None], kf[:, :, jnp.clip(t_i - s_i, 0, lc - 1), :], 0.0)
    tb = jnp.where((s_i >= t_i)[None, None, :, :, None], kb[:, :, jnp.clip(s_i - t_i, 0, lc - 1), :], 0.0)
    toep = jnp.transpose(tf + tb, (0, 2, 1, 3, 4)).reshape(g, blk, blk)
    rhs1 = jnp.concatenate([toep.astype(BF16), smat], axis=-1)

    a_r, a_i = pr[lc], pi[lc]
    atab = jnp.stack([jnp.concatenate([a_r, a_r], -1), jnp.concatenate([-a_i, a_i], -1),
                      jnp.concatenate([a_i, -a_i], -1)], axis=2)
    return rhs1, gcat, jnp.transpose(atab, (1, 0, 2, 3))


def _s5_kernel(u_ref, rhs_ref, g_ref, a_ref, o_ref, r_ref, *, gb, bsz, nch, ncc):
    p2 = 2 * S5_STATE
    for gi in range(gb):
        r_ref[gi] = _dot(u_ref[gi], rhs_ref[gi])
    a_tabs = [[[a_ref[gi, d, i:i + 1, :] for i in range(3)] for d in range(2)] for gi in range(gb)]

    def body(i, carry):
        cb = jnp.where(i < ncc, ncc - 1 - i, nch - 1 - (i - ncc))
        new = []
        for gi in range(gb):
            for d in range(2):
                h, hs = carry[2 * gi + d]
                c = i if d == 0 else cb
                rows = pl.ds(pl.multiple_of(c * bsz, bsz), bsz)
                base = S5_BLK + 2 * p2 * d
                s_in = r_ref[gi, rows, base:base + p2]
                s_sw = r_ref[gi, rows, base + p2:base + 2 * p2]
                r_ref[gi, rows, base:base + p2] = h
                a1, a2, a2s = a_tabs[gi][d]
                new.append((a1 * h + a2 * hs + s_in, a1 * hs + a2s * h + s_sw))
        return tuple(new)

    zero = jnp.zeros((bsz, p2), F32)
    lax.fori_loop(0, nch, body, tuple((zero, zero) for _ in range(2 * gb)))
    for gi in range(gb):
        hcat = jnp.concatenate([r_ref[gi, :, S5_BLK:S5_BLK + p2],
                                r_ref[gi, :, S5_BLK + 2 * p2:S5_BLK + 3 * p2]], axis=1).astype(BF16)
        o_ref[gi] = r_ref[gi, :, 0:S5_BLK] + _dot(hcat, g_ref[gi])


def _s5_scan(u, rhs1, gcat, atab, n_ctx):
    b, t, w = u.shape
    lc = S5_CHUNK
    nch = t // lc
    g = S5_GROUPS
    rows = nch * b
    ug = u.reshape(b, nch, lc, g, S5_GROUP).transpose(3, 1, 0, 2, 4).reshape(g, rows, S5_BLK).astype(BF16)
    gb = 2
    y = pl.pallas_call(
        functools.partial(_s5_kernel, gb=gb, bsz=b, nch=nch, ncc=n_ctx // lc),
        grid=(g // gb,),
        in_specs=[pl.BlockSpec((gb, rows, S5_BLK), lambda i: (i, 0, 0)),
                  pl.BlockSpec((gb, S5_BLK, 3 * S5_BLK), lambda i: (i, 0, 0)),
                  pl.BlockSpec((gb, S5_BLK, S5_BLK), lambda i: (i, 0, 0)),
                  pl.BlockSpec((gb, 2, 3, 2 * S5_STATE), lambda i: (i, 0, 0, 0))],
        out_specs=pl.BlockSpec((gb, rows, S5_BLK), lambda i: (i, 0, 0)),
        out_shape=jax.ShapeDtypeStruct((g, rows, S5_BLK), F32),
        scratch_shapes=[pltpu.VMEM((gb, rows, 3 * S5_BLK), F32)],
        compiler_params=_cparams(("arbitrary",), 48),
    )(ug, rhs1, gcat, atab)
    return y.reshape(g, nch, b, lc, S5_GROUP).transpose(2, 1, 3, 0, 4).reshape(b, t, w)


def _s5_glu_kernel(y_ref, u_ref, d_ref, w_ref, b_ref, o_ref):
    y = u_ref[0] * d_ref[...] + y_ref[0]
    y = 0.5 * y * (1.0 + jnp.tanh(math.sqrt(2.0 / math.pi) * (y + 0.044715 * (y * y * y))))
    gate = jax.nn.sigmoid(_dot(y.astype(BF16), w_ref[...]) + b_ref[...])
    o_ref[0] = (y * gate).astype(o_ref.dtype)


def _s5_glu(y, u, d_skip, w_glu, b_glu):
    b, t, w = y.shape
    tm = t // 8
    row = pl.BlockSpec((1, tm, w), lambda i, j: (i, j, 0))
    vec = pl.BlockSpec((1, w), lambda i, j: (0, 0))
    return pl.pallas_call(
        _s5_glu_kernel,
        grid=(b, t // tm),
        in_specs=[row, row, vec, pl.BlockSpec((w, w), lambda i, j: (0, 0)), vec],
        out_specs=row,
        out_shape=jax.ShapeDtypeStruct((b, t, w), BF16),
        compiler_params=_cparams(("arbitrary", "arbitrary"), 32),
    )(y, u, d_skip.reshape(1, w).astype(F32), w_glu, b_glu.reshape(1, w).astype(F32))


def _branch_out_kernel(ya_ref, yb_ref, yc_ref, g_ref, x_ref, wa_ref, wb_ref, wc_ref, wo_ref,
                       ml_ref, mc_ref, nw_ref, xo_ref, ho_ref, *, tm, n_ctx):
    d = D_MODEL
    t = pl.program_id(1)
    g = g_ref[0]
    mix = (g[:, 0:d].astype(F32) * _dot(ya_ref[0], wa_ref[...])
           + g[:, d:2 * d].astype(F32) * _dot(yb_ref[0], wb_ref[...])
           + g[:, 2 * d:3 * d].astype(F32) * _dot(yc_ref[0], wc_ref[...]))
    out = _dot(mix.astype(BF16), wo_ref[...])
    gate = _ctx_select(t, tm, n_ctx, mc_ref[2:3, :], ml_ref[0, 2:3, :])
    x1 = x_ref[0] + gate * out
    xo_ref[0] = x1
    shift = _ctx_select(t, tm, n_ctx, mc_ref[3:4, :], ml_ref[0, 3:4, :])
    scale = _ctx_select(t, tm, n_ctx, mc_ref[4:5, :], ml_ref[0, 4:5, :])
    ho_ref[0] = _rms_mod(x1, nw_ref[...], shift, scale).astype(ho_ref.dtype)


def _branch_out(ya, yb, yc, gates, x, wa, wb, wc, wo, m_lat, m_ctx, nw, n_ctx):
    b, t, d = x.shape
    tm = t // 8

    def row(n):
        return pl.BlockSpec((1, tm, n), lambda i, j: (i, j, 0))

    def full(a):
        return pl.BlockSpec(a.shape, lambda i, j: (0, 0))

    return pl.pallas_call(
        functools.partial(_branch_out_kernel, tm=tm, n_ctx=n_ctx),
        grid=(b, t // tm),
        in_specs=[row(ya.shape[2]), row(yb.shape[2]), row(yc.shape[2]), row(gates.shape[2]), row(d),
                  full(wa), full(wb), full(wc), full(wo),
                  pl.BlockSpec((1, 6, d), lambda i, j: (i, 0, 0)),
                  pl.BlockSpec((6, d), lambda i, j: (0, 0)),
                  pl.BlockSpec((1, d), lambda i, j: (0, 0))],
        out_specs=[row(d), row(d)],
        out_shape=[jax.ShapeDtypeStruct((b, t, d), F32), jax.ShapeDtypeStruct((b, t, d), BF16)],
        compiler_params=_cparams(("arbitrary", "arbitrary"), 56),
    )(ya, yb, yc, gates, x, wa, wb, wc, wo, m_lat, m_ctx, nw.reshape(1, d))


def _ffn2_kernel(a_ref, x_ref, w_ref, ml_ref, mc_ref, nl_ref, nc_ref, nw_ref, xo_ref, ho_ref, *, tm, n_ctx):
    t = pl.program_id(1)
    gate = _ctx_select(t, tm, n_ctx, mc_ref[5:6, :], ml_ref[0, 5:6, :])
    x2 = x_ref[0] + gate * _dot(a_ref[0], w_ref[...])
    xo_ref[0] = x2
    shift = _ctx_select(t, tm, n_ctx, nc_ref[0:1, :], nl_ref[0, 0:1, :])
    scale = _ctx_select(t, tm, n_ctx, nc_ref[1:2, :], nl_ref[0, 1:2, :])
    ho_ref[0] = _rms_mod(x2, nw_ref[...], shift, scale).astype(ho_ref.dtype)


def _ffn2(a, x, w, m_lat, m_ctx, next_lat, next_ctx, nw, n_ctx, h_dtype):
    b, t, d = x.shape
    tm = t // 8
    kf = a.shape[2]

    def row(n):
        return pl.BlockSpec((1, tm, n), lambda i, j: (i, j, 0))

    lat = pl.BlockSpec((1, 6, d), lambda i, j: (i, 0, 0))
    ctx = pl.BlockSpec((6, d), lambda i, j: (0, 0))
    return pl.pallas_call(
        functools.partial(_ffn2_kernel, tm=tm, n_ctx=n_ctx),
        grid=(b, t // tm),
        in_specs=[row(kf), row(d), pl.BlockSpec((kf, d), lambda i, j: (0, 0)), lat, ctx, lat, ctx,
                  pl.BlockSpec((1, d), lambda i, j: (0, 0))],
        out_specs=[row(d), row(d)],
        out_shape=[jax.ShapeDtypeStruct((b, t, d), F32), jax.ShapeDtypeStruct((b, t, d), h_dtype)],
        compiler_params=_cparams(("arbitrary", "arbitrary"), 56),
    )(a, x, w, m_lat, m_ctx, next_lat, next_ctx, nw.reshape(1, d))


def _expanders():
    heads = jnp.arange(SSD_HEADS)
    e64 = (jnp.arange(SSD_INNER)[None, :] // SSD_HEAD_DIM == heads[:, None]).astype(BF16)
    e128 = (jnp.arange(SSD_HEADS * LANES)[None, :] // LANES == heads[:, None]).astype(BF16)
    return e64, e128


def kernel(x, c, ctx, c_ctx, w_mod, b_mod, norm1_w, w_in, ssd_conv_w, ssd_conv_b, ssd_dt_bias, ssd_a_log, ssd_d, ssd_norm_w, s5_b_re, s5_b_im, s5_c_re, s5_c_im, s5_lam_re, s5_lam_im, s5_log_step, s5_d, s5_w_glu, s5_b_glu, da_lam_q1, da_lam_k1, da_lam_q2, da_lam_k2, da_subln_w, w_br_a, w_br_b, w_br_c, w_out, norm2_w, w_ff1, w_ff2, final_norm_w):
    bsz, n_lat, d = x.shape
    n_ctx = ctx.shape[1]
    depth = w_mod.shape[0]
    assert d == D_MODEL and n_ctx % SSD_CHUNK == 0 and n_lat % SSD_CHUNK == 0 and n_lat % GRID_W == 0
    t = n_ctx + n_lat
    assert t % 128 == 0

    rb = -(-(bsz + 1) // 8) * 8
    cvec = jnp.concatenate([c, c_ctx[None, :], jnp.zeros((rb - bsz - 1, d), F32)], axis=0)
    mods = _mods(cvec, w_mod, b_mod).reshape(depth, rb, 6, d)
    zeros_lat = jnp.zeros((bsz, 6, d), F32)
    zeros_ctx = jnp.zeros((6, d), F32)

    cos_t, sin_t = _rope_tables(n_ctx, n_lat)
    e64, e128 = _expanders()
    q_scale = DA_HEAD_DIM ** -0.5 * LOG2E
    offs = [0]
    for s in IN_SPLITS:
        offs.append(offs[-1] + s)

    xs = jnp.concatenate([ctx, x], axis=1)
    h = _norm_mod(xs, mods[0, :bsz], mods[0, bsz], norm1_w[0], n_ctx)
    out = None
    for i in range(depth):
        m_lat, m_ctx = mods[i, :bsz], mods[i, bsz]
        wi = w_in[i].astype(BF16)
        w_z, w_xbc, w_dt, w_u, w_q, w_k, w_v, w_g = (wi[:, offs[j]:offs[j + 1]] for j in range(8))
        z = _proj(h, w_z, BF16)
        xbc = _proj(h, w_xbc, F32)
        dt_raw = _proj(h, w_dt, F32)
        dt_raw_t = _proj_t(h, w_dt.T, F32)
        u = _proj(h, w_u, F32, tn=768)
        qk = _proj_rope(h, jnp.concatenate([w_q, w_k], axis=1), cos_t, sin_t, q_scale)
        vt = _proj_t(h, w_v.T, BF16)
        gates = _proj(h, w_g, BF16, act="sigmoid")

        xact = _conv_silu(xbc, ssd_conv_w[i].astype(F32), ssd_conv_b[i].astype(F32), n_ctx)
        yf = _ssd_direction(xact, dt_raw, dt_raw_t, ssd_dt_bias[i], ssd_a_log[i], e64, e128, n_ctx, False)
        yb = _ssd_direction(xact, dt_raw, dt_raw_t, ssd_dt_bias[i], ssd_a_log[i], e64, e128, n_ctx, True)
        y_a = _ssd_final(xact, yf, yb, z, ssd_d[i], ssd_norm_w[i])

        rhs1, gcat, atab = _s5_matrices(s5_b_re[i], s5_b_im[i], s5_c_re[i], s5_c_im[i],
                                        s5_lam_re[i], s5_lam_im[i], s5_log_step[i])
        y_s5 = _s5_scan(u, rhs1, gcat, atab, n_ctx)
        y_b = _s5_glu(y_s5, u, s5_d[i], s5_w_glu[i].astype(BF16), s5_b_glu[i])

        lam_init = 0.8 - 0.6 * math.exp(-0.3 * i)
        lam_params = jnp.stack([da_lam_q1[i], da_lam_k1[i], da_lam_q2[i], da_lam_k2[i]], axis=0).astype(F32)
        y_c = _attention(qk, vt, lam_params, da_subln_w[i].astype(F32), n_ctx, lam_init)

        x1, h2 = _branch_out(y_a, y_b, y_c, gates, xs, w_br_a[i].astype(BF16), w_br_b[i].astype(BF16),
                             w_br_c[i].astype(BF16), w_out[i].astype(BF16), m_lat, m_ctx, norm2_w[i], n_ctx)
        act = _proj(h2, w_ff1[i].astype(BF16), BF16, act="relu2")
        if i + 1 < depth:
            xs, h = _ffn2(act, x1, w_ff2[i].astype(BF16), m_lat, m_ctx, mods[i + 1, :bsz], mods[i + 1, bsz],
                          norm1_w[i + 1], n_ctx, BF16)
        else:
            _, out = _ffn2(act, x1, w_ff2[i].astype(BF16), m_lat, m_ctx, zeros_lat, zeros_ctx,
                           final_norm_w, n_ctx, F32)
    return out[:, n_ctx:]
```

```python
import functools
import math

import jax
import jax.numpy as jnp
from jax import lax
from jax.experimental import pallas as pl
from jax.experimental.pallas import tpu as pltpu

F32 = jnp.float32
BF16 = jnp.bfloat16

D_MODEL = 1024
EPS = 1e-6
GRID_W = 64
SSD_HEAD_DIM = 64
SSD_INNER = D_MODEL
SSD_HEADS = SSD_INNER // SSD_HEAD_DIM
SSD_GROUPS = 4
SSD_STATE = 64
SSD_CONV = 5
SSD_CHUNK = 128
SSD_BC = SSD_GROUPS * SSD_STATE
SSD_CONV_DIM = SSD_INNER + 2 * SSD_BC
S5_WIDTH = 3 * D_MODEL // 4
S5_GROUP = 16
S5_GROUPS = S5_WIDTH // S5_GROUP
S5_STATE = 64
S5_CHUNK = 16
S5_BLK = S5_CHUNK * S5_GROUP
DA_HEAD_DIM = 64
DA_V_DIM = 2 * DA_HEAD_DIM
DA_HEADS = D_MODEL // DA_V_DIM
ROPE_BASE = 10000.0
D_FF = 4 * D_MODEL
N_BRANCH = 3
IN_SPLITS = (SSD_INNER, SSD_CONV_DIM, 2 * SSD_HEADS, S5_WIDTH, D_MODEL, D_MODEL, D_MODEL, N_BRANCH * D_MODEL)

LANES = 128
MIB = 1024 * 1024
LOG2E = 1.4426950408889634
VT_PAD = 16
ATTN_KEY_CHUNK = 256

_HI = lax.Precision.HIGHEST


def _cparams(sem, vmem_mib):
    return pltpu.CompilerParams(dimension_semantics=sem, vmem_limit_bytes=int(vmem_mib * MIB))


def _dot(a, b):
    return jnp.dot(a, b, preferred_element_type=F32)


def _dot_nt(a, b):
    return lax.dot_general(a, b, (((1,), (1,)), ((), ())), preferred_element_type=F32)


def _ctx_select(t_idx, tm, n_ctx, ctx_row, lat_row):
    rows = t_idx * tm + lax.broadcasted_iota(jnp.int32, (tm, 1), 0)
    return jnp.where(rows < n_ctx, ctx_row, lat_row)


def _rms_mod(x, nw, shift, scale):
    y = x * lax.rsqrt(jnp.mean(x * x, axis=-1, keepdims=True) + EPS) * nw
    return y * (1.0 + scale) + shift


def _mods_kernel(s_ref, w_ref, b_ref, o_ref):
    s = s_ref[...]
    s = (s * jax.nn.sigmoid(s)).astype(BF16)
    o_ref[0] = _dot(s, w_ref[0]) + b_ref[0]


def _mods(cvec, w_mod, b_mod):
    depth, d, n = w_mod.shape
    rb = cvec.shape[0]
    tn = 1536
    return pl.pallas_call(
        _mods_kernel,
        grid=(depth, n // tn),
        in_specs=[pl.BlockSpec((rb, d), lambda i, j: (0, 0)),
                  pl.BlockSpec((1, d, tn), lambda i, j: (i, 0, j)),
                  pl.BlockSpec((1, 1, tn), lambda i, j: (i, 0, j))],
        out_specs=pl.BlockSpec((1, rb, tn), lambda i, j: (i, 0, j)),
        out_shape=jax.ShapeDtypeStruct((depth, rb, n), F32),
        compiler_params=_cparams(("arbitrary", "arbitrary"), 24),
    )(cvec, w_mod.astype(BF16), b_mod.reshape(depth, 1, n))


def _norm_mod_kernel(x_ref, ml_ref, mc_ref, nw_ref, o_ref, *, tm, n_ctx):
    t = pl.program_id(1)
    shift = _ctx_select(t, tm, n_ctx, mc_ref[0:1, :], ml_ref[0, 0:1, :])
    scale = _ctx_select(t, tm, n_ctx, mc_ref[1:2, :], ml_ref[0, 1:2, :])
    o_ref[0] = _rms_mod(x_ref[0], nw_ref[...], shift, scale).astype(o_ref.dtype)


def _norm_mod(x, m_lat, m_ctx, nw, n_ctx):
    b, t, d = x.shape
    tm = t // 8
    return pl.pallas_call(
        functools.partial(_norm_mod_kernel, tm=tm, n_ctx=n_ctx),
        grid=(b, t // tm),
        in_specs=[pl.BlockSpec((1, tm, d), lambda i, j: (i, j, 0)),
                  pl.BlockSpec((1, 6, d), lambda i, j: (i, 0, 0)),
                  pl.BlockSpec((6, d), lambda i, j: (0, 0)),
                  pl.BlockSpec((1, d), lambda i, j: (0, 0))],
        out_specs=pl.BlockSpec((1, tm, d), lambda i, j: (i, j, 0)),
        out_shape=jax.ShapeDtypeStruct((b, t, d), BF16),
        compiler_params=_cparams(("arbitrary", "arbitrary"), 32),
    )(x, m_lat, m_ctx, nw.reshape(1, d))


def _proj_kernel(x_ref, w_ref, o_ref, *, act):
    acc = _dot(x_ref[0], w_ref[...])
    if act == "sigmoid":
        acc = jax.nn.sigmoid(acc)
    elif act == "relu2":
        acc = jnp.maximum(acc, 0.0)
        acc = acc * acc
    o_ref[0] = acc.astype(o_ref.dtype)


def _proj(x, w, out_dtype, act=None, tn=512):
    b, t, k = x.shape
    n = w.shape[1]
    tn = min(tn, n)
    tm = t // 2
    return pl.pallas_call(
        functools.partial(_proj_kernel, act=act),
        grid=(b, t // tm, n // tn),
        in_specs=[pl.BlockSpec((1, tm, k), lambda i, j, l: (i, j, 0)),
                  pl.BlockSpec((k, tn), lambda i, j, l: (0, l))],
        out_specs=pl.BlockSpec((1, tm, tn), lambda i, j, l: (i, j, l)),
        out_shape=jax.ShapeDtypeStruct((b, t, n), out_dtype),
        compiler_params=_cparams(("arbitrary",) * 3, 48),
    )(x, w)


def _proj_t_kernel(x_ref, wt_ref, o_ref):
    o_ref[0] = _dot_nt(wt_ref[...], x_ref[0]).astype(o_ref.dtype)


def _proj_t(x, wt, out_dtype, tn=256):
    b, t, k = x.shape
    n = wt.shape[0]
    tn = min(tn, n)
    tm = t // 2
    return pl.pallas_call(
        _proj_t_kernel,
        grid=(b, t // tm, n // tn),
        in_specs=[pl.BlockSpec((1, tm, k), lambda i, j, l: (i, j, 0)),
                  pl.BlockSpec((tn, k), lambda i, j, l: (l, 0))],
        out_specs=pl.BlockSpec((1, tn, tm), lambda i, j, l: (i, l, j)),
        out_shape=jax.ShapeDtypeStruct((b, n, t), out_dtype),
        compiler_params=_cparams(("arbitrary",) * 3, 48),
    )(x, wt)


def _proj_rope_kernel(x_ref, w_ref, cos_ref, sin_ref, o_ref, *, tn, n_q_tiles, q_scale):
    acc = _dot(x_ref[0], w_ref[...])
    cos = cos_ref[...]
    sin = sin_ref[...]
    lane = lax.broadcasted_iota(jnp.int32, cos.shape, 1)
    first_half = (lane % 32) < 16
    scale = jnp.where(pl.program_id(2) < n_q_tiles, q_scale, 1.0).astype(F32)
    for c in range(tn // LANES):
        blk = acc[:, c * LANES:(c + 1) * LANES]
        partner = jnp.where(first_half, pltpu.roll(blk, LANES - 16, axis=1), pltpu.roll(blk, 16, axis=1))
        out = (blk * cos + partner * sin) * scale
        o_ref[0, :, c * LANES:(c + 1) * LANES] = out.astype(o_ref.dtype)


def _proj_rope(x, w, cos, sin, q_scale):
    b, t, k = x.shape
    n = w.shape[1]
    tn = 512
    tm = t // 2
    return pl.pallas_call(
        functools.partial(_proj_rope_kernel, tn=tn, n_q_tiles=(n // 2) // tn, q_scale=q_scale),
        grid=(b, t // tm, n // tn),
        in_specs=[pl.BlockSpec((1, tm, k), lambda i, j, l: (i, j, 0)),
                  pl.BlockSpec((k, tn), lambda i, j, l: (0, l)),
                  pl.BlockSpec((tm, LANES), lambda i, j, l: (j, 0)),
                  pl.BlockSpec((tm, LANES), lambda i, j, l: (j, 0))],
        out_specs=pl.BlockSpec((1, tm, tn), lambda i, j, l: (i, j, l)),
        out_shape=jax.ShapeDtypeStruct((b, t, n), BF16),
        compiler_params=_cparams(("arbitrary",) * 3, 48),
    )(x, w, cos, sin)


def _rope_tables(n_ctx, n_lat):
    rows = n_lat // GRID_W
    row = jnp.repeat(jnp.arange(rows, dtype=F32), GRID_W)
    col = jnp.tile(jnp.arange(GRID_W, dtype=F32), rows)
    axis_dim = DA_HEAD_DIM // 2
    inv_freq = ROPE_BASE ** (-jnp.arange(0, axis_dim, 2, dtype=F32) / axis_dim)
    ang = jnp.stack([row[:, None] * inv_freq, col[:, None] * inv_freq], axis=1)
    cos = jnp.cos(ang)
    sin = jnp.sin(ang)
    cos_h = jnp.concatenate([cos, cos], axis=-1).reshape(n_lat, DA_HEAD_DIM)
    sin_h = jnp.concatenate([-sin, sin], axis=-1).reshape(n_lat, DA_HEAD_DIM)
    cos_t = jnp.concatenate([jnp.ones((n_ctx, DA_HEAD_DIM), F32), cos_h], axis=0)
    sin_t = jnp.concatenate([jnp.zeros((n_ctx, DA_HEAD_DIM), F32), sin_h], axis=0)
    return jnp.tile(cos_t, (1, 2)), jnp.tile(sin_t, (1, 2))


def _attn_kernel(q_ref, k_ref, vt_ref, lamp_ref, sw_ref, o_ref, *, n_ctx, tq, kc, lam_init):
    qi = pl.program_id(2)
    lp = lamp_ref[...]
    lam = (jnp.exp(jnp.sum(lp[0:1] * lp[1:2], axis=-1, keepdims=True))
           - jnp.exp(jnp.sum(lp[2:3] * lp[3:4], axis=-1, keepdims=True)) + lam_init)
    q = q_ref[0]
    lane = lax.broadcasted_iota(jnp.int32, q.shape, 1)
    zero = jnp.zeros_like(q)
    q_both = jnp.concatenate([jnp.where(lane < DA_HEAD_DIM, q, zero), jnp.where(lane >= DA_HEAD_DIM, q, zero)],
                             axis=0)

    def attend(nk):
        s = _dot_nt(k_ref[0, 0:nk, :], q_both)
        slabs = [s[c0:c0 + kc] for c0 in range(0, nk, kc)]
        m = jnp.max(functools.reduce(jnp.maximum, slabs), axis=0, keepdims=True)
        acc = _dot(vt_ref[0, 0, :, 0:nk], jnp.exp2(s - m).astype(BF16))
        acc = acc[0:DA_V_DIM] * (1.0 / acc[DA_V_DIM:DA_V_DIM + 1])
        o = acc[:, 0:tq] - lam * acc[:, tq:2 * tq]
        o = o * lax.rsqrt(jnp.mean(o * o, axis=0, keepdims=True) + EPS)
        o = o * (sw_ref[...] * (1.0 - lam_init))
        o_ref[0] = o.T.astype(o_ref.dtype)

    n_ctx_tiles = n_ctx // tq

    @pl.when(qi < n_ctx_tiles)
    def _():
        attend(n_ctx)

    @pl.when(qi >= n_ctx_tiles)
    def _():
        attend(k_ref.shape[1])


def _proj_vt_kernel(x_ref, wt_ref, o_ref, *, heads):
    acc = _dot_nt(wt_ref[...], x_ref[0])
    tm = acc.shape[1]
    pad = (lax.broadcasted_iota(jnp.int32, (VT_PAD, tm), 0) == 0).astype(o_ref.dtype)
    for h in range(heads):
        o_ref[0, h, 0:DA_V_DIM, :] = acc[h * DA_V_DIM:(h + 1) * DA_V_DIM].astype(o_ref.dtype)
        o_ref[0, h, DA_V_DIM:DA_V_DIM + VT_PAD, :] = pad


def _proj_vt(x, wt):
    b, t, k = x.shape
    heads = 2
    tm = t // 2
    rows = DA_V_DIM + VT_PAD
    return pl.pallas_call(
        functools.partial(_proj_vt_kernel, heads=heads),
        grid=(b, t // tm, DA_HEADS // heads),
        in_specs=[pl.BlockSpec((1, tm, k), lambda i, j, l: (i, j, 0)),
                  pl.BlockSpec((heads * DA_V_DIM, k), lambda i, j, l: (l, 0))],
        out_specs=pl.BlockSpec((1, heads, rows, tm), lambda i, j, l: (i, l, 0, j)),
        out_shape=jax.ShapeDtypeStruct((b, DA_HEADS, rows, t), BF16),
        compiler_params=_cparams(("arbitrary",) * 3, 48),
    )(x, wt)


def _attention(qk, vt, lam_params, subln_w, n_ctx, lam_init):
    b, t, _ = qk.shape
    tq = 256 if n_ctx % 256 == 0 else 128
    return pl.pallas_call(
        functools.partial(_attn_kernel, n_ctx=n_ctx, tq=tq, kc=ATTN_KEY_CHUNK, lam_init=lam_init),
        grid=(b, DA_HEADS, t // tq),
        in_specs=[pl.BlockSpec((1, tq, LANES), lambda i, h, j: (i, j, h)),
                  pl.BlockSpec((1, t, LANES), lambda i, h, j: (i, 0, DA_HEADS + h)),
                  pl.BlockSpec((1, 1, DA_V_DIM + VT_PAD, t), lambda i, h, j: (i, h, 0, 0)),
                  pl.BlockSpec((4, DA_HEAD_DIM), lambda i, h, j: (0, 0)),
                  pl.BlockSpec((DA_V_DIM, 1), lambda i, h, j: (0, 0))],
        out_specs=pl.BlockSpec((1, tq, LANES), lambda i, h, j: (i, j, h)),
        out_shape=jax.ShapeDtypeStruct((b, t, D_MODEL), BF16),
        compiler_params=_cparams(("arbitrary",) * 3, 56),
    )(qk, qk, vt, lam_params, subln_w.reshape(DA_V_DIM, 1))


def _conv_silu_kernel(x_ref, w_ref, b_ref, o_ref, *, n_ctx):
    x = x_ref[0]
    t = x.shape[0]
    rows = lax.broadcasted_iota(jnp.int32, (t, 1), 0)
    is_ctx = rows < n_ctx
    lo = jnp.where(is_ctx, 0, n_ctx)
    hi = jnp.where(is_ctx, n_ctx, t)
    acc = x * w_ref[SSD_CONV // 2:SSD_CONV // 2 + 1, :] + b_ref[...]
    for j in range(-(SSD_CONV // 2), SSD_CONV // 2 + 1):
        if j == 0:
            continue
        src = rows + j
        shifted = pltpu.roll(x, (-j) % t, axis=0)
        valid = (src >= lo) & (src < hi)
        acc = acc + jnp.where(valid, shifted, 0.0) * w_ref[j + SSD_CONV // 2:j + SSD_CONV // 2 + 1, :]
    o_ref[0] = acc * jax.nn.sigmoid(acc)


def _conv_silu(xbc, conv_w, conv_b, n_ctx):
    b, t, c = xbc.shape
    tc = 256
    return pl.pallas_call(
        functools.partial(_conv_silu_kernel, n_ctx=n_ctx),
        grid=(b, c // tc),
        in_specs=[pl.BlockSpec((1, t, tc), lambda i, j: (i, 0, j)),
                  pl.BlockSpec((SSD_CONV, tc), lambda i, j: (0, j)),
                  pl.BlockSpec((1, tc), lambda i, j: (0, j))],
        out_specs=pl.BlockSpec((1, t, tc), lambda i, j: (i, 0, j)),
        out_shape=jax.ShapeDtypeStruct((b, t, c), F32),
        compiler_params=_cparams(("arbitrary", "arbitrary"), 56),
    )(xbc, conv_w, conv_b.reshape(1, c))


def _softplus(x):
    return jnp.maximum(x, 0.0) + jnp.log1p(jnp.exp(-jnp.abs(x)))


def _expand(v, e_ref, pieces):
    e = e_ref[...]
    out = None
    rem = v
    for _ in range(pieces):
        part = rem.astype(BF16)
        term = _dot(part, e)
        out = term if out is None else out + term
        rem = rem - part.astype(F32)
    return out


def _ssd_kernel(x_ref, dt_ref, dtt_ref, pr_ref, pc_ref, e64_ref, e128_ref, o_ref, h_ref, *, reverse, d):
    q = SSD_CHUNK
    nh = SSD_HEADS

    @pl.when(pl.program_id(1) == 0)
    def _():
        h_ref[...] = jnp.zeros_like(h_ref)

    xa = x_ref[0]
    xh = xa[:, :SSD_INNER]
    bm = xa[:, SSD_INNER:SSD_INNER + SSD_BC]
    cm = xa[:, SSD_INNER + SSD_BC:]
    bias_r = pr_ref[0:1, :]
    a_r = -jnp.exp(pr_ref[1:2, :])
    bias_c = pc_ref[:, 0:1]
    a_c = -jnp.exp(pc_ref[:, 1:2])
    dt = _softplus(dt_ref[0][:, d * nh:(d + 1) * nh] + bias_r)
    dtt = _softplus(dtt_ref[0][d * nh:(d + 1) * nh, :] + bias_c)
    da = dt * a_r
    dat = dtt * a_c
    ri = lax.broadcasted_iota(jnp.int32, (q, q), 0)
    ci = lax.broadcasted_iota(jnp.int32, (q, q), 1)
    keep = (ci >= ri) if reverse else (ci <= ri)
    tri = keep.astype(F32)
    acum = jnp.dot(tri, da, precision=_HI, preferred_element_type=F32)
    acumt = lax.dot_general(dat, tri, (((1,), (1,)), ((), ())), precision=_HI,
                            preferred_element_type=F32)
    total = acum[0:1, :] if reverse else acum[q - 1:q, :]
    to_end = jnp.exp(total - acum)
    eacum = jnp.exp(acum)
    cdecay = jnp.broadcast_to(jnp.exp(total), (8, nh))
    dt_x = _expand(dt, e64_ref, 2)
    te_x = _expand(to_end, e64_ref, 2)
    ea_x = _expand(eacum, e64_ref, 3)
    cd_x = _expand(cdecay, e64_ref, 3)[0:1, :]
    ac_x = _expand(acum, e128_ref, 3)

    xs = xh * dt_x
    xs_b = xs.astype(BF16)
    xe_b = (xs * te_x).astype(BF16)
    lane = lax.broadcasted_iota(jnp.int32, (q, LANES), 1)
    lo_half = lane < SSD_STATE
    zero_b = jnp.zeros((q, LANES), BF16)

    y_cols = []
    for g in range(SSD_GROUPS):
        pair = g // 2
        bm_p = bm[:, pair * LANES:(pair + 1) * LANES]
        cm_p = cm[:, pair * LANES:(pair + 1) * LANES]
        mine = lo_half if g % 2 == 0 else jnp.logical_not(lo_half)
        cz = jnp.where(mine, cm_p, 0.0).astype(BF16)
        bt = bm_p.T.astype(BF16)
        cb = _dot(cz, bt)
        cols = slice(g * 4 * SSD_HEAD_DIM, (g + 1) * 4 * SSD_HEAD_DIM)
        hp = h_ref[g]
        y_off = _dot(cz, hp.astype(BF16)) * ea_x[:, cols]
        h_ref[g] = hp * cd_x[:, cols] + _dot(bt, xe_b[:, cols])
        y_parts = []
        for hpair in range(2):
            ws = []
            xin = []
            for sub in range(2):
                hh = g * 4 + hpair * 2 + sub
                seg = ac_x[:, hh * LANES:(hh + 1) * LANES] - acumt[hh:hh + 1, :]
                dec = jnp.exp(jnp.where(keep, seg, -jnp.inf))
                ws.append((cb * dec).astype(BF16))
                xp = xs_b[:, (hh // 2) * LANES:(hh // 2 + 1) * LANES]
                xin.append(jnp.where(lo_half if sub == 0 else jnp.logical_not(lo_half), xp, zero_b))
            y_parts.append(_dot(jnp.concatenate(ws, axis=1), jnp.concatenate(xin, axis=0)))
        y_cols.append(jnp.concatenate(y_parts, axis=1) + y_off)
    o_ref[0] = jnp.concatenate(y_cols, axis=1)


def _ssd_direction(xact, dt_raw, dt_raw_t, dt_bias, a_log, e64, e128, n_ctx, reverse):
    b, t, c = xact.shape
    q = SSD_CHUNK
    nc = t // q
    ncc = n_ctx // q
    d = 1 if reverse else 0
    if reverse:
        def cidx(j):
            return jnp.where(j < ncc, ncc - 1 - j, nc - 1 - (j - ncc))
    else:
        def cidx(j):
            return j
    pr = jnp.stack([dt_bias[d], a_log[d]], axis=0).astype(F32)
    return pl.pallas_call(
        functools.partial(_ssd_kernel, reverse=reverse, d=d),
        grid=(b, nc),
        in_specs=[pl.BlockSpec((1, q, c), lambda i, j: (i, cidx(j), 0)),
                  pl.BlockSpec((1, q, 2 * SSD_HEADS), lambda i, j: (i, cidx(j), 0)),
                  pl.BlockSpec((1, 2 * SSD_HEADS, q), lambda i, j: (i, 0, cidx(j))),
                  pl.BlockSpec((2, SSD_HEADS), lambda i, j: (0, 0)),
                  pl.BlockSpec((SSD_HEADS, 2), lambda i, j: (0, 0)),
                  pl.BlockSpec(e64.shape, lambda i, j: (0, 0)),
                  pl.BlockSpec(e128.shape, lambda i, j: (0, 0))],
        out_specs=pl.BlockSpec((1, q, SSD_INNER), lambda i, j: (i, cidx(j), 0)),
        out_shape=jax.ShapeDtypeStruct((b, t, SSD_INNER), F32),
        scratch_shapes=[pltpu.VMEM((SSD_GROUPS, 2 * SSD_STATE, 4 * SSD_HEAD_DIM), F32)],
        compiler_params=_cparams(("arbitrary", "arbitrary"), 32),
    )(xact, dt_raw, dt_raw_t, pr, pr.T, e64, e128)


def _ssd_final_kernel(x_ref, yf_ref, yb_ref, z_ref, d_ref, nw_ref, o_ref):
    y = x_ref[0] * d_ref[...] + yf_ref[0] + yb_ref[0]
    z = z_ref[0].astype(F32)
    y = y * (z * jax.nn.sigmoid(z))
    gw = SSD_INNER // SSD_GROUPS
    for g in range(SSD_GROUPS):
        yg = y[:, g * gw:(g + 1) * gw]
        yg = yg * lax.rsqrt(jnp.mean(yg * yg, axis=-1, keepdims=True) + EPS)
        o_ref[0, :, g * gw:(g + 1) * gw] = (yg * nw_ref[:, g * gw:(g + 1) * gw]).astype(o_ref.dtype)


def _ssd_final(xact, yf, yb, z, d_skip, norm_w):
    b, t, _ = yf.shape
    tm = t // 8
    n = SSD_INNER
    row = pl.BlockSpec((1, tm, n), lambda i, j: (i, j, 0))
    vec = pl.BlockSpec((1, n), lambda i, j: (0, 0))
    return pl.pallas_call(
        _ssd_final_kernel,
        grid=(b, t // tm),
        in_specs=[row, row, row, row, vec, vec],
        out_specs=row,
        out_shape=jax.ShapeDtypeStruct((b, t, n), BF16),
        compiler_params=_cparams(("arbitrary", "arbitrary"), 48),
    )(xact, yf, yb, z, jnp.repeat(d_skip.astype(F32), SSD_HEAD_DIM).reshape(1, n), norm_w.reshape(1, n))


def _s5_prep_kernel(bt_ref, cw_ref, vx_ref, btt_ref, gx_ref, ctt_ref, kall_ref, s_ref, g_ref):
    for d in range(2):
        kall_ref[0, d] = jnp.dot(bt_ref[0], cw_ref[0, d], precision=_HI, preferred_element_type=F32)
    bre = btt_ref[0, 0]
    bim = btt_ref[0, 1]
    cre = ctt_ref[0, 0]
    cim = ctt_ref[0, 1]
    p = S5_STATE
    for d in range(2):
        vr = vx_ref[0, d, 0]
        vi = vx_ref[0, d, 1]
        sr = (vr * bre - vi * bim).astype(BF16)
        si = (vr * bim + vi * bre).astype(BF16)
        s_ref[0, :, (4 * d + 0) * p:(4 * d + 1) * p] = sr
        s_ref[0, :, (4 * d + 1) * p:(4 * d + 2) * p] = si
        s_ref[0, :, (4 * d + 2) * p:(4 * d + 3) * p] = si
        s_ref[0, :, (4 * d + 3) * p:(4 * d + 4) * p] = sr
        gr = gx_ref[0, d, 0]
        gi = gx_ref[0, d, 1]
        g_ref[0, (2 * d) * p:(2 * d + 1) * p, :] = (cre * gr - cim * gi).astype(BF16)
        g_ref[0, (2 * d + 1) * p:(2 * d + 2) * p, :] = (-(cre * gi + cim * gr)).astype(BF16)


def _s5_matrices(b_re, b_im, c_re, c_im, lam_re, lam_im, log_step):
    g, p, kk = b_re.shape
    lc = S5_CHUNK
    b_re, b_im, c_re, c_im = (a.astype(F32) for a in (b_re, b_im, c_re, c_im))
    lr = jnp.minimum(lam_re.astype(F32), -1e-4)
    li = lam_im.astype(F32)
    step = jnp.exp(log_step.astype(F32))[..., None]
    mag = jnp.exp(lr * step)
    ar, ai = mag * jnp.cos(li * step), mag * jnp.sin(li * step)
    den = lr * lr + li * li
    fr = ((ar - 1) * lr + ai * li) / den
    fi = (ai * lr - (ar - 1) * li) / den
    jj = jnp.arange(lc + 1, dtype=F32)[:, None, None, None]
    pmag = jnp.exp(jj * (lr * step)[None])
    pr = pmag * jnp.cos(jj * (li * step)[None])
    pi = pmag * jnp.sin(jj * (li * step)[None])
    wr = pr * fr[None] - pi * fi[None]
    wi = pr * fi[None] + pi * fr[None]

    def rep_cols(a):
        return jnp.repeat(jnp.transpose(a, (1, 2, 0)), kk, axis=-1)

    def rep_rows(a):
        return jnp.repeat(jnp.transpose(a, (1, 0, 2)), kk, axis=1)

    ctt_re = jnp.tile(jnp.transpose(c_re, (0, 2, 1)), (1, 1, lc))
    ctt_im = jnp.tile(jnp.transpose(c_im, (0, 2, 1)), (1, 1, lc))
    btt_re = jnp.tile(jnp.transpose(b_re, (0, 2, 1)), (1, lc, 1))
    btt_im = jnp.tile(jnp.transpose(b_im, (0, 2, 1)), (1, lc, 1))
    bt = jnp.concatenate([jnp.transpose(b_re, (0, 2, 1)), jnp.transpose(b_im, (0, 2, 1))], axis=-1)

    cw, vx, gx = [], [], []
    for d in range(2):
        lag = slice(0, lc)
        wjr, wji = wr[lag, d], wi[lag, d]
        wxr, wxi = rep_cols(wjr), rep_cols(wji)
        cwr = ctt_re * wxr - ctt_im * wxi
        cwi = ctt_re * wxi + ctt_im * wxr
        cw.append(jnp.concatenate([cwr, -cwi], axis=1))
        vr_s, vi_s = (wr[lag, d][::-1], wi[lag, d][::-1]) if d == 0 else (wr[lag, d], wi[lag, d])
        vx.append(jnp.stack([rep_rows(vr_s), rep_rows(vi_s)], axis=1))
        gr_t, gi_t = (pr[1:, d], pi[1:, d]) if d == 0 else (pr[1:, d][::-1], pi[1:, d][::-1])
        gx.append(jnp.stack([rep_cols(gr_t), rep_cols(gi_t)], axis=1))
    cw = jnp.stack(cw, axis=1)
    vx = jnp.stack(vx, axis=1)
    gx = jnp.stack(gx, axis=1)
    btt = jnp.stack([btt_re, btt_im], axis=1)
    ctt = jnp.stack([ctt_re, ctt_im], axis=1)

    blk = S5_BLK
    kall, smat, gcat = pl.pallas_call(
        _s5_prep_kernel,
        grid=(g,),
        in_specs=[pl.BlockSpec((1, kk, 2 * p), lambda i: (i, 0, 0)),
                  pl.BlockSpec((1, 2, 2 * p, blk), lambda i: (i, 0, 0, 0)),
                  pl.BlockSpec((1, 2, 2, blk, p), lambda i: (i, 0, 0, 0, 0)),
                  pl.BlockSpec((1, 2, blk, p), lambda i: (i, 0, 0, 0)),
                  pl.BlockSpec((1, 2, 2, p, blk), lambda i: (i, 0, 0, 0, 0)),
                  pl.BlockSpec((1, 2, p, blk), lambda i: (i, 0, 0, 0))],
        out_specs=[pl.BlockSpec((1, 2, kk, blk), lambda i: (i, 0, 0, 0)),
                   pl.BlockSpec((1, blk, 8 * p), lambda i: (i, 0, 0)),
                   pl.BlockSpec((1, 4 * p, blk), lambda i: (i, 0, 0))],
        out_shape=[jax.ShapeDtypeStruct((g, 2, kk, blk), F32),
                   jax.ShapeDtypeStruct((g, blk, 8 * p), BF16),
                   jax.ShapeDtypeStruct((g, 4 * p, blk), BF16)],
        compiler_params=_cparams(("arbitrary",), 24),
    )(bt, cw, vx, btt, gx, ctt)

    kf = kall[:, 0].reshape(g, kk, lc, kk)
    kb = kall[:, 1].reshape(g, kk, lc, kk)
    s_i = jnp.arange(lc)[:, None]
    t_i = jnp.arange(lc)[None, :]
    tf = jnp.where((t_i >= s_i)[None, None, :, :, None], kf[:, :, jnp.clip(t_i - s_i, 0, lc - 1), :], 0.0)
    tb = jnp.where((s_i >= t_i)[None, None, :, :, None], kb[:, :, jnp.clip(s_i - t_i, 0, lc - 1), :], 0.0)
    toep = jnp.transpose(tf + tb, (0, 2, 1, 3, 4)).reshape(g, blk, blk)
    rhs1 = jnp.concatenate([toep.astype(BF16), smat], axis=-1)

    a_r, a_i = pr[lc], pi[lc]
    atab = jnp.stack([jnp.concatenate([a_r, a_r], -1), jnp.concatenate([-a_i, a_i], -1),
                      jnp.concatenate([a_i, -a_i], -1)], axis=2)
    return rhs1, gcat, jnp.transpose(atab, (1, 0, 2, 3))


def _s5_kernel(u_ref, rhs_ref, g_ref, a_ref, o_ref, r_ref, *, gb, bsz, nch, ncc):
    p2 = 2 * S5_STATE
    for gi in range(gb):
        r_ref[gi] = _dot(u_ref[gi], rhs_ref[gi])
    a_tabs = [[[a_ref[gi, d, i:i + 1, :] for i in range(3)] for d in range(2)] for gi in range(gb)]

    def body(i, carry):
        cb = jnp.where(i < ncc, ncc - 1 - i, nch - 1 - (i - ncc))
        new = []
        for gi in range(gb):
            for d in range(2):
                h, hs = carry[2 * gi + d]
                c = i if d == 0 else cb
                rows = pl.ds(pl.multiple_of(c * bsz, bsz), bsz)
                base = S5_BLK + 2 * p2 * d
                s_in = r_ref[gi, rows, base:base + p2]
                s_sw = r_ref[gi, rows, base + p2:base + 2 * p2]
                r_ref[gi, rows, base:base + p2] = h
                a1, a2, a2s = a_tabs[gi][d]
                new.append((a1 * h + a2 * hs + s_in, a1 * hs + a2s * h + s_sw))
        return tuple(new)

    zero = jnp.zeros((bsz, p2), F32)
    lax.fori_loop(0, nch, body, tuple((zero, zero) for _ in range(2 * gb)))
    for gi in range(gb):
        hcat = jnp.concatenate([r_ref[gi, :, S5_BLK:S5_BLK + p2],
                                r_ref[gi, :, S5_BLK + 2 * p2:S5_BLK + 3 * p2]], axis=1).astype(BF16)
        o_ref[gi] = r_ref[gi, :, 0:S5_BLK] + _dot(hcat, g_ref[gi])


def _proj_tm_kernel(x_ref, w_ref, o_ref):
    acc = _dot(x_ref[0], w_ref[...])
    tm, n = acc.shape
    o_ref[:, 0] = acc.reshape(tm // S5_CHUNK, S5_CHUNK, n)


def _proj_tm(x, w):
    b, t, k = x.shape
    n = w.shape[1]
    tm = t // 2
    return pl.pallas_call(
        _proj_tm_kernel,
        grid=(b, t // tm),
        in_specs=[pl.BlockSpec((1, tm, k), lambda i, j: (i, j, 0)),
                  pl.BlockSpec((k, n), lambda i, j: (0, 0))],
        out_specs=pl.BlockSpec((tm // S5_CHUNK, 1, S5_CHUNK, n), lambda i, j: (j, i, 0, 0)),
        out_shape=jax.ShapeDtypeStruct((t // S5_CHUNK, b, S5_CHUNK, n), F32),
        compiler_params=_cparams(("arbitrary", "arbitrary"), 48),
    )(x, w)


def _lane_group(shape):
    return lax.broadcasted_iota(jnp.int32, shape, 1) // S5_GROUP


def _s5_pack_kernel(x_ref, o_ref, *, rt):
    sub = 16
    grp = _lane_group((sub, LANES))
    per = LANES // S5_GROUP
    for r0 in range(0, rt, sub):
        xs = [x_ref[pl.ds(r0 * S5_CHUNK + s, sub, stride=S5_CHUNK), :] for s in range(S5_CHUNK)]
        for g in range(per):
            for j in range(S5_CHUNK // per):
                col = jnp.zeros((sub, LANES), F32)
                for s8 in range(per):
                    shift = ((s8 - g) % per) * S5_GROUP
                    v = xs[per * j + s8]
                    v = pltpu.roll(v, shift, axis=1) if shift else v
                    col = jnp.where(grp == s8, v, col)
                o_ref[g, r0:r0 + sub, j * LANES:(j + 1) * LANES] = col.astype(o_ref.dtype)


def _s5_pack(u2, rows):
    w = u2.shape[1]
    per = LANES // S5_GROUP
    units = rows // 16
    rt = 16 * max(dv for dv in range(1, 18) if units % dv == 0)
    return pl.pallas_call(
        functools.partial(_s5_pack_kernel, rt=rt),
        grid=(w // LANES, rows // rt),
        in_specs=[pl.BlockSpec((rt * S5_CHUNK, LANES), lambda c, r: (r, c))],
        out_specs=pl.BlockSpec((per, rt, S5_BLK), lambda c, r: (c, r, 0)),
        out_shape=jax.ShapeDtypeStruct((S5_GROUPS, rows, S5_BLK), BF16),
        compiler_params=_cparams(("arbitrary", "arbitrary"), 32),
    )(u2)


def _s5_scan(ug, rhs1, gcat, atab, b, nch, n_ctx):
    lc = S5_CHUNK
    g = S5_GROUPS
    rows = nch * b
    gb = 2
    return pl.pallas_call(
        functools.partial(_s5_kernel, gb=gb, bsz=b, nch=nch, ncc=n_ctx // lc),
        grid=(g // gb,),
        in_specs=[pl.BlockSpec((gb, rows, S5_BLK), lambda i: (i, 0, 0)),
                  pl.BlockSpec((gb, S5_BLK, 3 * S5_BLK), lambda i: (i, 0, 0)),
                  pl.BlockSpec((gb, S5_BLK, S5_BLK), lambda i: (i, 0, 0)),
                  pl.BlockSpec((gb, 2, 3, 2 * S5_STATE), lambda i: (i, 0, 0, 0))],
        out_specs=pl.BlockSpec((gb, rows, S5_BLK), lambda i: (i, 0, 0)),
        out_shape=jax.ShapeDtypeStruct((g, rows, S5_BLK), F32),
        scratch_shapes=[pltpu.VMEM((gb, rows, 3 * S5_BLK), F32)],
        compiler_params=_cparams(("arbitrary",), 48),
    )(ug, rhs1, gcat, atab)


def _s5_glu_kernel(y_ref, u_ref, d_ref, w_ref, b_ref, o_ref, ybuf_ref, *, rt):
    sub = 8
    per = LANES // S5_GROUP
    grp = _lane_group((sub, LANES))
    for r0 in range(0, rt, sub):
        for cb in range(S5_GROUPS // per):
            ys = [[y_ref[cb * per + g, r0:r0 + sub, j * LANES:(j + 1) * LANES] for j in range(S5_CHUNK // per)]
                  for g in range(per)]
            for t in range(S5_CHUNK):
                col = jnp.zeros((sub, LANES), F32)
                for g in range(per):
                    shift = ((g - t % per) % per) * S5_GROUP
                    v = ys[g][t // per]
                    v = pltpu.roll(v, shift, axis=1) if shift else v
                    col = jnp.where(grp == g, v, col)
                ybuf_ref[cb, pl.ds(r0 * S5_CHUNK + t, sub, stride=S5_CHUNK), :] = col
    y = u_ref[...] * d_ref[...] + jnp.concatenate([ybuf_ref[cb] for cb in range(S5_GROUPS // per)], axis=1)
    y = 0.5 * y * (1.0 + jnp.tanh(math.sqrt(2.0 / math.pi) * (y + 0.044715 * (y * y * y))))
    gate = jax.nn.sigmoid(_dot(y.astype(BF16), w_ref[...]) + b_ref[...])
    o_ref[...] = (y * gate).astype(o_ref.dtype)


def _s5_glu(yg, u2, d_skip, w_glu, b_glu, b):
    g, rows, _ = yg.shape
    w = u2.shape[1]
    rt = 8 * b
    tok = pl.BlockSpec((rt * S5_CHUNK, w), lambda r: (r, 0))
    vec = pl.BlockSpec((1, w), lambda r: (0, 0))
    return pl.pallas_call(
        functools.partial(_s5_glu_kernel, rt=rt),
        grid=(rows // rt,),
        in_specs=[pl.BlockSpec((g, rt, S5_BLK), lambda r: (0, r, 0)), tok, vec,
                  pl.BlockSpec((w, w), lambda r: (0, 0)), vec],
        out_specs=tok,
        out_shape=jax.ShapeDtypeStruct(u2.shape, BF16),
        scratch_shapes=[pltpu.VMEM((w // LANES, rt * S5_CHUNK, LANES), F32)],
        compiler_params=_cparams(("arbitrary",), 40),
    )(yg, u2, d_skip.reshape(1, w).astype(F32), w_glu, b_glu.reshape(1, w).astype(F32))


def _branch_out_kernel(ya_ref, yb_ref, yc_ref, g_ref, x_ref, wa_ref, wb_ref, wc_ref, wo_ref,
                       ml_ref, mc_ref, nw_ref, xo_ref, ho_ref, *, tm, n_ctx):
    d = D_MODEL
    t = pl.program_id(1)
    g = g_ref[0]
    yb = yb_ref[:, 0].reshape(tm, yb_ref.shape[3])
    mix = (g[:, 0:d].astype(F32) * _dot(ya_ref[0], wa_ref[...])
           + g[:, d:2 * d].astype(F32) * _dot(yb, wb_ref[...])
           + g[:, 2 * d:3 * d].astype(F32) * _dot(yc_ref[0], wc_ref[...]))
    out = _dot(mix.astype(BF16), wo_ref[...])
    gate = _ctx_select(t, tm, n_ctx, mc_ref[2:3, :], ml_ref[0, 2:3, :])
    x1 = x_ref[0] + gate * out
    xo_ref[0] = x1
    shift = _ctx_select(t, tm, n_ctx, mc_ref[3:4, :], ml_ref[0, 3:4, :])
    scale = _ctx_select(t, tm, n_ctx, mc_ref[4:5, :], ml_ref[0, 4:5, :])
    ho_ref[0] = _rms_mod(x1, nw_ref[...], shift, scale).astype(ho_ref.dtype)


def _branch_out(ya, yb, yc, gates, x, wa, wb, wc, wo, m_lat, m_ctx, nw, n_ctx):
    b, t, d = x.shape
    tm = t // 8

    def row(n):
        return pl.BlockSpec((1, tm, n), lambda i, j: (i, j, 0))

    def full(a):
        return pl.BlockSpec(a.shape, lambda i, j: (0, 0))

    return pl.pallas_call(
        functools.partial(_branch_out_kernel, tm=tm, n_ctx=n_ctx),
        grid=(b, t // tm),
        in_specs=[row(ya.shape[2]),
                  pl.BlockSpec((tm // S5_CHUNK, 1, S5_CHUNK, yb.shape[3]), lambda i, j: (j, i, 0, 0)),
                  row(yc.shape[2]), row(gates.shape[2]), row(d),
                  full(wa), full(wb), full(wc), full(wo),
                  pl.BlockSpec((1, 6, d), lambda i, j: (i, 0, 0)),
                  pl.BlockSpec((6, d), lambda i, j: (0, 0)),
                  pl.BlockSpec((1, d), lambda i, j: (0, 0))],
        out_specs=[row(d), row(d)],
        out_shape=[jax.ShapeDtypeStruct((b, t, d), F32), jax.ShapeDtypeStruct((b, t, d), BF16)],
        compiler_params=_cparams(("arbitrary", "arbitrary"), 56),
    )(ya, yb, yc, gates, x, wa, wb, wc, wo, m_lat, m_ctx, nw.reshape(1, d))


def _ffn2_kernel(a_ref, x_ref, w_ref, ml_ref, mc_ref, nl_ref, nc_ref, nw_ref, xo_ref, ho_ref, *, tm, n_ctx):
    t = pl.program_id(1)
    gate = _ctx_select(t, tm, n_ctx, mc_ref[5:6, :], ml_ref[0, 5:6, :])
    x2 = x_ref[0] + gate * _dot(a_ref[0], w_ref[...])
    xo_ref[0] = x2
    shift = _ctx_select(t, tm, n_ctx, nc_ref[0:1, :], nl_ref[0, 0:1, :])
    scale = _ctx_select(t, tm, n_ctx, nc_ref[1:2, :], nl_ref[0, 1:2, :])
    ho_ref[0] = _rms_mod(x2, nw_ref[...], shift, scale).astype(ho_ref.dtype)


def _ffn2(a, x, w, m_lat, m_ctx, next_lat, next_ctx, nw, n_ctx, h_dtype):
    b, t, d = x.shape
    tm = t // 8
    kf = a.shape[2]

    def row(n):
        return pl.BlockSpec((1, tm, n), lambda i, j: (i, j, 0))

    lat = pl.BlockSpec((1, 6, d), lambda i, j: (i, 0, 0))
    ctx = pl.BlockSpec((6, d), lambda i, j: (0, 0))
    return pl.pallas_call(
        functools.partial(_ffn2_kernel, tm=tm, n_ctx=n_ctx),
        grid=(b, t // tm),
        in_specs=[row(kf), row(d), pl.BlockSpec((kf, d), lambda i, j: (0, 0)), lat, ctx, lat, ctx,
                  pl.BlockSpec((1, d), lambda i, j: (0, 0))],
        out_specs=[row(d), row(d)],
        out_shape=[jax.ShapeDtypeStruct((b, t, d), F32), jax.ShapeDtypeStruct((b, t, d), h_dtype)],
        compiler_params=_cparams(("arbitrary", "arbitrary"), 56),
    )(a, x, w, m_lat, m_ctx, next_lat, next_ctx, nw.reshape(1, d))


def _expanders():
    heads = jnp.arange(SSD_HEADS)
    e64 = (jnp.arange(SSD_INNER)[None, :] // SSD_HEAD_DIM == heads[:, None]).astype(BF16)
    e128 = (jnp.arange(SSD_HEADS * LANES)[None, :] // LANES == heads[:, None]).astype(BF16)
    return e64, e128


def kernel(x, c, ctx, c_ctx, w_mod, b_mod, norm1_w, w_in, ssd_conv_w, ssd_conv_b, ssd_dt_bias, ssd_a_log, ssd_d, ssd_norm_w, s5_b_re, s5_b_im, s5_c_re, s5_c_im, s5_lam_re, s5_lam_im, s5_log_step, s5_d, s5_w_glu, s5_b_glu, da_lam_q1, da_lam_k1, da_lam_q2, da_lam_k2, da_subln_w, w_br_a, w_br_b, w_br_c, w_out, norm2_w, w_ff1, w_ff2, final_norm_w):
    bsz, n_lat, d = x.shape
    n_ctx = ctx.shape[1]
    depth = w_mod.shape[0]
    assert d == D_MODEL and n_ctx % SSD_CHUNK == 0 and n_lat % SSD_CHUNK == 0 and n_lat % GRID_W == 0
    t = n_ctx + n_lat
    assert t % 128 == 0

    rb = -(-(bsz + 1) // 8) * 8
    cvec = jnp.concatenate([c, c_ctx[None, :], jnp.zeros((rb - bsz - 1, d), F32)], axis=0)
    mods = _mods(cvec, w_mod, b_mod).reshape(depth, rb, 6, d)
    zeros_lat = jnp.zeros((bsz, 6, d), F32)
    zeros_ctx = jnp.zeros((6, d), F32)

    cos_t, sin_t = _rope_tables(n_ctx, n_lat)
    e64, e128 = _expanders()
    q_scale = DA_HEAD_DIM ** -0.5 * LOG2E
    offs = [0]
    for s in IN_SPLITS:
        offs.append(offs[-1] + s)

    xs = jnp.concatenate([ctx, x], axis=1)
    h = _norm_mod(xs, mods[0, :bsz], mods[0, bsz], norm1_w[0], n_ctx)
    out = None
    for i in range(depth):
        m_lat, m_ctx = mods[i, :bsz], mods[i, bsz]
        wi = w_in[i].astype(BF16)
        w_z, w_xbc, w_dt, w_u, w_q, w_k, w_v, w_g = (wi[:, offs[j]:offs[j + 1]] for j in range(8))
        z = _proj(h, w_z, BF16)
        xbc = _proj(h, w_xbc, F32)
        dt_raw = _proj(h, w_dt, F32)
        dt_raw_t = _proj_t(h, w_dt.T, F32)
        u2 = _proj_tm(h, w_u).reshape(t * bsz, S5_WIDTH)
        qk = _proj_rope(h, jnp.concatenate([w_q, w_k], axis=1), cos_t, sin_t, q_scale)
        vt = _proj_vt(h, w_v.T)
        gates = _proj(h, w_g, BF16, act="sigmoid")

        xact = _conv_silu(xbc, ssd_conv_w[i].astype(F32), ssd_conv_b[i].astype(F32), n_ctx)
        yf = _ssd_direction(xact, dt_raw, dt_raw_t, ssd_dt_bias[i], ssd_a_log[i], e64, e128, n_ctx, False)
        yb = _ssd_direction(xact, dt_raw, dt_raw_t, ssd_dt_bias[i], ssd_a_log[i], e64, e128, n_ctx, True)
        y_a = _ssd_final(xact, yf, yb, z, ssd_d[i], ssd_norm_w[i])

        rhs1, gcat, atab = _s5_matrices(s5_b_re[i], s5_b_im[i], s5_c_re[i], s5_c_im[i],
                                        s5_lam_re[i], s5_lam_im[i], s5_log_step[i])
        nch = t // S5_CHUNK
        y_s5 = _s5_scan(_s5_pack(u2, nch * bsz), rhs1, gcat, atab, bsz, nch, n_ctx)
        y_b = _s5_glu(y_s5, u2, s5_d[i], s5_w_glu[i].astype(BF16), s5_b_glu[i], bsz)
        y_b = y_b.reshape(nch, bsz, S5_CHUNK, S5_WIDTH)

        lam_init = 0.8 - 0.6 * math.exp(-0.3 * i)
        lam_params = jnp.stack([da_lam_q1[i], da_lam_k1[i], da_lam_q2[i], da_lam_k2[i]], axis=0).astype(F32)
        y_c = _attention(qk, vt, lam_params, da_subln_w[i].astype(F32), n_ctx, lam_init)

        x1, h2 = _branch_out(y_a, y_b, y_c, gates, xs, w_br_a[i].astype(BF16), w_br_b[i].astype(BF16),
                             w_br_c[i].astype(BF16), w_out[i].astype(BF16), m_lat, m_ctx, norm2_w[i], n_ctx)
        act = _proj(h2, w_ff1[i].astype(BF16), BF16, act="relu2")
        if i + 1 < depth:
            xs, h = _ffn2(act, x1, w_ff2[i].astype(BF16), m_lat, m_ctx, mods[i + 1, :bsz], mods[i + 1, bsz],
                          norm1_w[i + 1], n_ctx, BF16)
        else:
            _, out = _ffn2(act, x1, w_ff2[i].astype(BF16), m_lat, m_ctx, zeros_lat, zeros_ctx,
                           final_norm_w, n_ctx, F32)
    return out[:, n_ctx:]
```

```python
import functools
import math

import jax
import jax.numpy as jnp
from jax import lax
from jax.experimental import pallas as pl
from jax.experimental.pallas import tpu as pltpu

F32 = jnp.float32
BF16 = jnp.bfloat16

D_MODEL = 1024
EPS = 1e-6
GRID_W = 64
SSD_HEAD_DIM = 64
SSD_INNER = D_MODEL
SSD_HEADS = SSD_INNER // SSD_HEAD_DIM
SSD_GROUPS = 4
SSD_STATE = 64
SSD_CONV = 5
SSD_CHUNK = 128
SSD_BC = SSD_GROUPS * SSD_STATE
SSD_CONV_DIM = SSD_INNER + 2 * SSD_BC
S5_WIDTH = 3 * D_MODEL // 4
S5_GROUP = 16
S5_GROUPS = S5_WIDTH // S5_GROUP
S5_STATE = 64
S5_CHUNK = 16
S5_BLK = S5_CHUNK * S5_GROUP
DA_HEAD_DIM = 64
DA_V_DIM = 2 * DA_HEAD_DIM
DA_HEADS = D_MODEL // DA_V_DIM
ROPE_BASE = 10000.0
D_FF = 4 * D_MODEL
N_BRANCH = 3
IN_SPLITS = (SSD_INNER, SSD_CONV_DIM, 2 * SSD_HEADS, S5_WIDTH, D_MODEL, D_MODEL, D_MODEL, N_BRANCH * D_MODEL)

LANES = 128
MIB = 1024 * 1024
LOG2E = 1.4426950408889634
VT_PAD = 16
ATTN_KEY_CHUNK = 256
ATTN_SPANS = 4

_HI = lax.Precision.HIGHEST


def _cparams(sem, vmem_mib):
    return pltpu.CompilerParams(dimension_semantics=sem, vmem_limit_bytes=int(vmem_mib * MIB))


def _dot(a, b):
    return jnp.dot(a, b, preferred_element_type=F32)


def _dot_nt(a, b):
    return lax.dot_general(a, b, (((1,), (1,)), ((), ())), preferred_element_type=F32)


def _ctx_select(t_idx, tm, n_ctx, ctx_row, lat_row):
    rows = t_idx * tm + lax.broadcasted_iota(jnp.int32, (tm, 1), 0)
    return jnp.where(rows < n_ctx, ctx_row, lat_row)


def _rms_mod(x, nw, shift, scale):
    y = x * lax.rsqrt(jnp.mean(x * x, axis=-1, keepdims=True) + EPS) * nw
    return y * (1.0 + scale) + shift


def _mods_kernel(s_ref, w_ref, b_ref, o_ref):
    s = s_ref[...]
    s = (s * jax.nn.sigmoid(s)).astype(BF16)
    o_ref[0] = _dot(s, w_ref[0]) + b_ref[0]


def _mods(cvec, w_mod, b_mod):
    depth, d, n = w_mod.shape
    rb = cvec.shape[0]
    tn = 1536
    return pl.pallas_call(
        _mods_kernel,
        grid=(depth, n // tn),
        in_specs=[pl.BlockSpec((rb, d), lambda i, j: (0, 0)),
                  pl.BlockSpec((1, d, tn), lambda i, j: (i, 0, j)),
                  pl.BlockSpec((1, 1, tn), lambda i, j: (i, 0, j))],
        out_specs=pl.BlockSpec((1, rb, tn), lambda i, j: (i, 0, j)),
        out_shape=jax.ShapeDtypeStruct((depth, rb, n), F32),
        compiler_params=_cparams(("arbitrary", "arbitrary"), 24),
    )(cvec, w_mod.astype(BF16), b_mod.reshape(depth, 1, n))


def _norm_mod_kernel(x_ref, ml_ref, mc_ref, nw_ref, o_ref, *, tm, n_ctx):
    t = pl.program_id(1)
    shift = _ctx_select(t, tm, n_ctx, mc_ref[0:1, :], ml_ref[0, 0:1, :])
    scale = _ctx_select(t, tm, n_ctx, mc_ref[1:2, :], ml_ref[0, 1:2, :])
    o_ref[0] = _rms_mod(x_ref[0], nw_ref[...], shift, scale).astype(o_ref.dtype)


def _norm_mod(x, m_lat, m_ctx, nw, n_ctx):
    b, t, d = x.shape
    tm = t // 8
    return pl.pallas_call(
        functools.partial(_norm_mod_kernel, tm=tm, n_ctx=n_ctx),
        grid=(b, t // tm),
        in_specs=[pl.BlockSpec((1, tm, d), lambda i, j: (i, j, 0)),
                  pl.BlockSpec((1, 6, d), lambda i, j: (i, 0, 0)),
                  pl.BlockSpec((6, d), lambda i, j: (0, 0)),
                  pl.BlockSpec((1, d), lambda i, j: (0, 0))],
        out_specs=pl.BlockSpec((1, tm, d), lambda i, j: (i, j, 0)),
        out_shape=jax.ShapeDtypeStruct((b, t, d), BF16),
        compiler_params=_cparams(("arbitrary", "arbitrary"), 32),
    )(x, m_lat, m_ctx, nw.reshape(1, d))


def _proj_kernel(x_ref, w_ref, o_ref, *, act):
    acc = _dot(x_ref[0], w_ref[...])
    if act == "sigmoid":
        acc = jax.nn.sigmoid(acc)
    elif act == "relu2":
        acc = jnp.maximum(acc, 0.0)
        acc = acc * acc
    o_ref[0] = acc.astype(o_ref.dtype)


def _proj(x, w, out_dtype, act=None, tn=1024):
    b, t, k = x.shape
    n = w.shape[1]
    tn = min(tn, n)
    tm = t // 2
    return pl.pallas_call(
        functools.partial(_proj_kernel, act=act),
        grid=(b, t // tm, n // tn),
        in_specs=[pl.BlockSpec((1, tm, k), lambda i, j, l: (i, j, 0)),
                  pl.BlockSpec((k, tn), lambda i, j, l: (0, l))],
        out_specs=pl.BlockSpec((1, tm, tn), lambda i, j, l: (i, j, l)),
        out_shape=jax.ShapeDtypeStruct((b, t, n), out_dtype),
        compiler_params=_cparams(("arbitrary",) * 3, 56),
    )(x, w)


def _proj_t_kernel(x_ref, wt_ref, o_ref):
    o_ref[0] = _dot_nt(wt_ref[...], x_ref[0]).astype(o_ref.dtype)


def _proj_t(x, wt, out_dtype, tn=256):
    b, t, k = x.shape
    n = wt.shape[0]
    tn = min(tn, n)
    tm = t // 2
    return pl.pallas_call(
        _proj_t_kernel,
        grid=(b, t // tm, n // tn),
        in_specs=[pl.BlockSpec((1, tm, k), lambda i, j, l: (i, j, 0)),
                  pl.BlockSpec((tn, k), lambda i, j, l: (l, 0))],
        out_specs=pl.BlockSpec((1, tn, tm), lambda i, j, l: (i, l, j)),
        out_shape=jax.ShapeDtypeStruct((b, n, t), out_dtype),
        compiler_params=_cparams(("arbitrary",) * 3, 48),
    )(x, wt)


def _proj_rope_kernel(x_ref, w_ref, cos_ref, sin_ref, o_ref, *, tn, n_q_tiles, q_scale):
    acc = _dot(x_ref[0], w_ref[...])
    cos = cos_ref[...]
    sin = sin_ref[...]
    lane = lax.broadcasted_iota(jnp.int32, cos.shape, 1)
    first_half = (lane % 32) < 16
    scale = jnp.where(pl.program_id(2) < n_q_tiles, q_scale, 1.0).astype(F32)
    for c in range(tn // LANES):
        blk = acc[:, c * LANES:(c + 1) * LANES]
        partner = jnp.where(first_half, pltpu.roll(blk, LANES - 16, axis=1), pltpu.roll(blk, 16, axis=1))
        out = (blk * cos + partner * sin) * scale
        o_ref[0, :, c * LANES:(c + 1) * LANES] = out.astype(o_ref.dtype)


def _proj_rope(x, w, cos, sin, q_scale):
    b, t, k = x.shape
    n = w.shape[1]
    tn = 512
    tm = t // 2
    return pl.pallas_call(
        functools.partial(_proj_rope_kernel, tn=tn, n_q_tiles=(n // 2) // tn, q_scale=q_scale),
        grid=(b, t // tm, n // tn),
        in_specs=[pl.BlockSpec((1, tm, k), lambda i, j, l: (i, j, 0)),
                  pl.BlockSpec((k, tn), lambda i, j, l: (0, l)),
                  pl.BlockSpec((tm, LANES), lambda i, j, l: (j, 0)),
                  pl.BlockSpec((tm, LANES), lambda i, j, l: (j, 0))],
        out_specs=pl.BlockSpec((1, tm, tn), lambda i, j, l: (i, j, l)),
        out_shape=jax.ShapeDtypeStruct((b, t, n), BF16),
        compiler_params=_cparams(("arbitrary",) * 3, 48),
    )(x, w, cos, sin)


def _rope_tables(n_ctx, n_lat):
    rows = n_lat // GRID_W
    row = jnp.repeat(jnp.arange(rows, dtype=F32), GRID_W)
    col = jnp.tile(jnp.arange(GRID_W, dtype=F32), rows)
    axis_dim = DA_HEAD_DIM // 2
    inv_freq = ROPE_BASE ** (-jnp.arange(0, axis_dim, 2, dtype=F32) / axis_dim)
    ang = jnp.stack([row[:, None] * inv_freq, col[:, None] * inv_freq], axis=1)
    cos = jnp.cos(ang)
    sin = jnp.sin(ang)
    cos_h = jnp.concatenate([cos, cos], axis=-1).reshape(n_lat, DA_HEAD_DIM)
    sin_h = jnp.concatenate([-sin, sin], axis=-1).reshape(n_lat, DA_HEAD_DIM)
    cos_t = jnp.concatenate([jnp.ones((n_ctx, DA_HEAD_DIM), F32), cos_h], axis=0)
    sin_t = jnp.concatenate([jnp.zeros((n_ctx, DA_HEAD_DIM), F32), sin_h], axis=0)
    return jnp.tile(cos_t, (1, 2)), jnp.tile(sin_t, (1, 2))


def _attn_kernel(q_ref, k_ref, vt_ref, lamp_ref, sw_ref, o_ref, *scratch, n_ctx, tq, kc, lam_init):
    s_refs = scratch[:len(scratch) // 2]
    p_refs = scratch[len(scratch) // 2:]
    qi = pl.program_id(2)
    lp = lamp_ref[...]
    lam = (jnp.exp(jnp.sum(lp[0:1] * lp[1:2], axis=-1, keepdims=True))
           - jnp.exp(jnp.sum(lp[2:3] * lp[3:4], axis=-1, keepdims=True)) + lam_init)
    q = q_ref[0]
    lane = lax.broadcasted_iota(jnp.int32, q.shape, 1)
    zero = jnp.zeros_like(q)
    q_both = jnp.concatenate([jnp.where(lane < DA_HEAD_DIM, q, zero), jnp.where(lane >= DA_HEAD_DIM, q, zero)],
                             axis=0)

    def attend(nk):
        n_slab = nk // kc
        n_span = min(ATTN_SPANS, n_slab)
        per = -(-n_slab // n_span) * kc
        spans = [(c0, min(c0 + per, nk)) for c0 in range(0, nk, per)]

        def scores(i):
            c0, c1 = spans[i]
            mx = None
            for r0 in range(c0, c1, kc):
                sv = _dot_nt(k_ref[0, r0:r0 + kc, :], q_both)
                s_refs[i][r0 - c0:r0 - c0 + kc, :] = sv
                sm = jnp.max(sv.reshape(kc // 8, 8, sv.shape[1]), axis=0)
                mx = sm if mx is None else jnp.maximum(mx, sm)
            return jnp.max(mx, axis=0, keepdims=True)

        parts = []
        mc_next = scores(0)
        for i, (c0, c1) in enumerate(spans):
            mc = mc_next
            if i + 1 < len(spans):
                mc_next = scores(i + 1)
            for r0 in range(0, c1 - c0, kc):
                p_refs[i][r0:r0 + kc, :] = jnp.exp2(s_refs[i][r0:r0 + kc, :] - mc).astype(BF16)
            parts.append((mc, _dot(vt_ref[0, 0, :, c0:c1], p_refs[i][0:c1 - c0, :])))
        m = functools.reduce(jnp.maximum, [mc for mc, _ in parts])
        acc = functools.reduce(lambda a, b: a + b, [a * jnp.exp2(mc - m) for mc, a in parts])
        acc = acc[0:DA_V_DIM] * (1.0 / acc[DA_V_DIM:DA_V_DIM + 1])
        o = acc[:, 0:tq] - lam * acc[:, tq:2 * tq]
        o = o * lax.rsqrt(jnp.mean(o * o, axis=0, keepdims=True) + EPS)
        o = o * (sw_ref[...] * (1.0 - lam_init))
        o_ref[0] = o.T.astype(o_ref.dtype)

    n_ctx_tiles = n_ctx // tq

    @pl.when(qi < n_ctx_tiles)
    def _():
        attend(n_ctx)

    @pl.when(qi >= n_ctx_tiles)
    def _():
        attend(k_ref.shape[1])


def _proj_vt_kernel(x_ref, wt_ref, o_ref, *, heads):
    acc = _dot_nt(wt_ref[...], x_ref[0])
    tm = acc.shape[1]
    pad = (lax.broadcasted_iota(jnp.int32, (VT_PAD, tm), 0) == 0).astype(o_ref.dtype)
    for h in range(heads):
        o_ref[0, h, 0:DA_V_DIM, :] = acc[h * DA_V_DIM:(h + 1) * DA_V_DIM].astype(o_ref.dtype)
        o_ref[0, h, DA_V_DIM:DA_V_DIM + VT_PAD, :] = pad


def _proj_vt(x, wt):
    b, t, k = x.shape
    heads = 2
    tm = t // 2
    rows = DA_V_DIM + VT_PAD
    return pl.pallas_call(
        functools.partial(_proj_vt_kernel, heads=heads),
        grid=(b, t // tm, DA_HEADS // heads),
        in_specs=[pl.BlockSpec((1, tm, k), lambda i, j, l: (i, j, 0)),
                  pl.BlockSpec((heads * DA_V_DIM, k), lambda i, j, l: (l, 0))],
        out_specs=pl.BlockSpec((1, heads, rows, tm), lambda i, j, l: (i, l, 0, j)),
        out_shape=jax.ShapeDtypeStruct((b, DA_HEADS, rows, t), BF16),
        compiler_params=_cparams(("arbitrary",) * 3, 48),
    )(x, wt)


def _attention(qk, vt, lam_params, subln_w, n_ctx, lam_init):
    b, t, _ = qk.shape
    tq = 256 if n_ctx % 256 == 0 else 128
    span = -(-(t // ATTN_KEY_CHUNK) // ATTN_SPANS) * ATTN_KEY_CHUNK
    return pl.pallas_call(
        functools.partial(_attn_kernel, n_ctx=n_ctx, tq=tq, kc=ATTN_KEY_CHUNK, lam_init=lam_init),
        grid=(b, DA_HEADS, t // tq),
        in_specs=[pl.BlockSpec((1, tq, LANES), lambda i, h, j: (i, j, h)),
                  pl.BlockSpec((1, t, LANES), lambda i, h, j: (i, 0, DA_HEADS + h)),
                  pl.BlockSpec((1, 1, DA_V_DIM + VT_PAD, t), lambda i, h, j: (i, h, 0, 0)),
                  pl.BlockSpec((4, DA_HEAD_DIM), lambda i, h, j: (0, 0)),
                  pl.BlockSpec((DA_V_DIM, 1), lambda i, h, j: (0, 0))],
        out_specs=pl.BlockSpec((1, tq, LANES), lambda i, h, j: (i, j, h)),
        out_shape=jax.ShapeDtypeStruct((b, t, D_MODEL), BF16),
        scratch_shapes=[pltpu.VMEM((span, 2 * tq), F32)] * ATTN_SPANS + [pltpu.VMEM((span, 2 * tq), BF16)] * ATTN_SPANS,
        compiler_params=_cparams(("arbitrary",) * 3, 56),
    )(qk, qk, vt, lam_params, subln_w.reshape(DA_V_DIM, 1))


def _conv_silu_kernel(x_ref, w_ref, b_ref, o_ref, *, n_ctx):
    x = x_ref[0]
    t = x.shape[0]
    rows = lax.broadcasted_iota(jnp.int32, (t, 1), 0)
    is_ctx = rows < n_ctx
    lo = jnp.where(is_ctx, 0, n_ctx)
    hi = jnp.where(is_ctx, n_ctx, t)
    acc = x * w_ref[SSD_CONV // 2:SSD_CONV // 2 + 1, :] + b_ref[...]
    for j in range(-(SSD_CONV // 2), SSD_CONV // 2 + 1):
        if j == 0:
            continue
        src = rows + j
        shifted = pltpu.roll(x, (-j) % t, axis=0)
        valid = (src >= lo) & (src < hi)
        acc = acc + jnp.where(valid, shifted, 0.0) * w_ref[j + SSD_CONV // 2:j + SSD_CONV // 2 + 1, :]
    o_ref[0] = acc * jax.nn.sigmoid(acc)


def _conv_silu(xbc, conv_w, conv_b, n_ctx):
    b, t, c = xbc.shape
    tc = 256
    return pl.pallas_call(
        functools.partial(_conv_silu_kernel, n_ctx=n_ctx),
        grid=(b, c // tc),
        in_specs=[pl.BlockSpec((1, t, tc), lambda i, j: (i, 0, j)),
                  pl.BlockSpec((SSD_CONV, tc), lambda i, j: (0, j)),
                  pl.BlockSpec((1, tc), lambda i, j: (0, j))],
        out_specs=pl.BlockSpec((1, t, tc), lambda i, j: (i, 0, j)),
        out_shape=jax.ShapeDtypeStruct((b, t, c), F32),
        compiler_params=_cparams(("arbitrary", "arbitrary"), 56),
    )(xbc, conv_w, conv_b.reshape(1, c))


def _softplus(x):
    return jnp.maximum(x, 0.0) + jnp.log1p(jnp.exp(-jnp.abs(x)))


def _expand(v, e_ref, pieces):
    e = e_ref[...]
    out = None
    rem = v
    for _ in range(pieces):
        part = rem.astype(BF16)
        term = _dot(part, e)
        out = term if out is None else out + term
        rem = rem - part.astype(F32)
    return out


def _ssd_prep(x_ref, dt_ref, dtt_ref, pr_ref, pc_ref, e128_ref, reverse, d):
    q = SSD_CHUNK
    nh = SSD_HEADS
    xa = x_ref[0]
    xh = xa[:, :SSD_INNER]
    bm = xa[:, SSD_INNER:SSD_INNER + SSD_BC]
    cm = xa[:, SSD_INNER + SSD_BC:]
    bias_r = pr_ref[0:1, :]
    a_r = -jnp.exp(pr_ref[1:2, :])
    bias_c = pc_ref[:, 0:1]
    a_c = -jnp.exp(pc_ref[:, 1:2])
    dt = _softplus(dt_ref[0][:, d * nh:(d + 1) * nh] + bias_r)
    dtt = _softplus(dtt_ref[0][d * nh:(d + 1) * nh, :] + bias_c)
    da = dt * a_r
    dat = dtt * a_c
    ri = lax.broadcasted_iota(jnp.int32, (q, q), 0)
    ci = lax.broadcasted_iota(jnp.int32, (q, q), 1)
    keep = (ci >= ri) if reverse else (ci <= ri)
    tri = keep.astype(F32)
    acum = jnp.dot(tri, da, precision=_HI, preferred_element_type=F32)
    acumt = lax.dot_general(dat, tri, (((1,), (1,)), ((), ())), precision=_HI,
                            preferred_element_type=F32)
    dt_w = _expand(dt, e128_ref, 2)
    ac_w = _expand(acum, e128_ref, 2)
    lane = lax.broadcasted_iota(jnp.int32, (q, LANES), 1)
    lo_half = lane < SSD_STATE
    zero_b = jnp.zeros((q, LANES), BF16)

    def head_pair(fn, i):
        return jnp.where(lo_half, fn(2 * i), fn(2 * i + 1))

    def wide(a, hh):
        return a[:, hh * LANES:(hh + 1) * LANES]

    def total_of(hh):
        return wide(ac_w, hh)[0:1, :] if reverse else wide(ac_w, hh)[q - 1:q, :]

    n_pair = nh // 2
    dt_x = jnp.concatenate([head_pair(lambda hh: wide(dt_w, hh), i) for i in range(n_pair)], axis=1)
    te_x = jnp.concatenate([head_pair(lambda hh: jnp.exp(total_of(hh) - wide(ac_w, hh)), i)
                            for i in range(n_pair)], axis=1)
    ea_x = jnp.concatenate([head_pair(lambda hh: jnp.exp(wide(ac_w, hh)), i) for i in range(n_pair)], axis=1)
    cd_x = jnp.concatenate([jnp.where(lo_half[0:1], jnp.exp(total_of(2 * i)), jnp.exp(total_of(2 * i + 1)))
                            for i in range(n_pair)], axis=1)
    ac_x = ac_w

    xs = xh * dt_x
    xs_b = xs.astype(BF16)
    xe_b = (xs * te_x).astype(BF16)
    return bm, cm, keep, acumt, ac_x, ea_x, cd_x, xs_b, xe_b


def _ssd_groups(prep, o_ref, h_ref):
    bm, cm, keep, acumt, ac_x, ea_x, cd_x, xs_b, xe_b = prep
    q = SSD_CHUNK
    lane = lax.broadcasted_iota(jnp.int32, (q, LANES), 1)
    lo_half = lane < SSD_STATE
    zero_b = jnp.zeros((q, LANES), BF16)
    y_cols = []
    for g in range(SSD_GROUPS):
        pair = g // 2
        bm_p = bm[:, pair * LANES:(pair + 1) * LANES]
        cm_p = cm[:, pair * LANES:(pair + 1) * LANES]
        mine = lo_half if g % 2 == 0 else jnp.logical_not(lo_half)
        cz = jnp.where(mine, cm_p, 0.0).astype(BF16)
        bt = bm_p.T.astype(BF16)
        cb = _dot(cz, bt)
        cols = slice(g * 4 * SSD_HEAD_DIM, (g + 1) * 4 * SSD_HEAD_DIM)
        hp = h_ref[g]
        y_off = _dot(cz, hp.astype(BF16)) * ea_x[:, cols]
        h_ref[g] = hp * cd_x[:, cols] + _dot(bt, xe_b[:, cols])
        y_parts = []
        for hpair in range(2):
            ws = []
            xin = []
            for sub in range(2):
                hh = g * 4 + hpair * 2 + sub
                seg = ac_x[:, hh * LANES:(hh + 1) * LANES] - acumt[hh:hh + 1, :]
                dec = jnp.exp(jnp.where(keep, seg, -jnp.inf))
                ws.append((cb * dec).astype(BF16))
                xp = xs_b[:, (hh // 2) * LANES:(hh // 2 + 1) * LANES]
                xin.append(jnp.where(lo_half if sub == 0 else jnp.logical_not(lo_half), xp, zero_b))
            y_parts.append(_dot(jnp.concatenate(ws, axis=1), jnp.concatenate(xin, axis=0)))
        y_cols.append(jnp.concatenate(y_parts, axis=1) + y_off)
    o_ref[0] = jnp.concatenate(y_cols, axis=1)


def _ssd_kernel(xf_ref, dtf_ref, dttf_ref, xb_ref, dtb_ref, dttb_ref, prf_ref, pcf_ref, prb_ref, pcb_ref,
                e128_ref, of_ref, ob_ref, h_ref):
    @pl.when(pl.program_id(1) == 0)
    def _():
        h_ref[...] = jnp.zeros_like(h_ref)

    prep_f = _ssd_prep(xf_ref, dtf_ref, dttf_ref, prf_ref, pcf_ref, e128_ref, False, 0)
    prep_b = _ssd_prep(xb_ref, dtb_ref, dttb_ref, prb_ref, pcb_ref, e128_ref, True, 1)
    _ssd_groups(prep_f, of_ref, h_ref.at[0])
    _ssd_groups(prep_b, ob_ref, h_ref.at[1])


def _ssd_scan(xact, dt_raw, dt_raw_t, dt_bias, a_log, e128, n_ctx):
    b, t, c = xact.shape
    q = SSD_CHUNK
    nc = t // q
    ncc = n_ctx // q

    def fwd(j):
        return j

    def bwd(j):
        return jnp.where(j < ncc, ncc - 1 - j, nc - 1 - (j - ncc))

    def specs(cidx):
        return [pl.BlockSpec((1, q, c), lambda i, j: (i, cidx(j), 0)),
                pl.BlockSpec((1, q, 2 * SSD_HEADS), lambda i, j: (i, cidx(j), 0)),
                pl.BlockSpec((1, 2 * SSD_HEADS, q), lambda i, j: (i, 0, cidx(j)))]

    prs = [jnp.stack([dt_bias[d], a_log[d]], axis=0).astype(F32) for d in range(2)]
    small = [pl.BlockSpec((2, SSD_HEADS), lambda i, j: (0, 0)), pl.BlockSpec((SSD_HEADS, 2), lambda i, j: (0, 0))]
    out_f = pl.BlockSpec((1, q, SSD_INNER), lambda i, j: (i, fwd(j), 0))
    out_b = pl.BlockSpec((1, q, SSD_INNER), lambda i, j: (i, bwd(j), 0))
    return pl.pallas_call(
        _ssd_kernel,
        grid=(b, nc),
        in_specs=specs(fwd) + specs(bwd) + small + small + [pl.BlockSpec(e128.shape, lambda i, j: (0, 0))],
        out_specs=[out_f, out_b],
        out_shape=[jax.ShapeDtypeStruct((b, t, SSD_INNER), F32)] * 2,
        scratch_shapes=[pltpu.VMEM((2, SSD_GROUPS, 2 * SSD_STATE, 4 * SSD_HEAD_DIM), F32)],
        compiler_params=_cparams(("arbitrary", "arbitrary"), 40),
    )(xact, dt_raw, dt_raw_t, xact, dt_raw, dt_raw_t, prs[0], prs[0].T, prs[1], prs[1].T, e128)


def _ssd_final_kernel(x_ref, yf_ref, yb_ref, z_ref, d_ref, nw_ref, o_ref):
    y = x_ref[0] * d_ref[...] + yf_ref[0] + yb_ref[0]
    z = z_ref[0].astype(F32)
    y = y * (z * jax.nn.sigmoid(z))
    gw = SSD_INNER // SSD_GROUPS
    for g in range(SSD_GROUPS):
        yg = y[:, g * gw:(g + 1) * gw]
        yg = yg * lax.rsqrt(jnp.mean(yg * yg, axis=-1, keepdims=True) + EPS)
        o_ref[0, :, g * gw:(g + 1) * gw] = (yg * nw_ref[:, g * gw:(g + 1) * gw]).astype(o_ref.dtype)


def _ssd_final(xact, yf, yb, z, d_skip, norm_w):
    b, t, _ = yf.shape
    tm = t // 8
    n = SSD_INNER
    row = pl.BlockSpec((1, tm, n), lambda i, j: (i, j, 0))
    vec = pl.BlockSpec((1, n), lambda i, j: (0, 0))
    return pl.pallas_call(
        _ssd_final_kernel,
        grid=(b, t // tm),
        in_specs=[row, row, row, row, vec, vec],
        out_specs=row,
        out_shape=jax.ShapeDtypeStruct((b, t, n), BF16),
        compiler_params=_cparams(("arbitrary", "arbitrary"), 48),
    )(xact, yf, yb, z, jnp.repeat(d_skip.astype(F32), SSD_HEAD_DIM).reshape(1, n), norm_w.reshape(1, n))


def _s5_prep_kernel(bt_ref, cw_ref, vx_ref, btt_ref, gx_ref, ctt_ref, kall_ref, s_ref, g_ref):
    for d in range(2):
        kall_ref[0, d] = jnp.dot(bt_ref[0], cw_ref[0, d], precision=_HI, preferred_element_type=F32)
    bre = btt_ref[0, 0]
    bim = btt_ref[0, 1]
    cre = ctt_ref[0, 0]
    cim = ctt_ref[0, 1]
    p = S5_STATE
    for d in range(2):
        vr = vx_ref[0, d, 0]
        vi = vx_ref[0, d, 1]
        sr = (vr * bre - vi * bim).astype(BF16)
        si = (vr * bim + vi * bre).astype(BF16)
        s_ref[0, :, (4 * d + 0) * p:(4 * d + 1) * p] = sr
        s_ref[0, :, (4 * d + 1) * p:(4 * d + 2) * p] = si
        s_ref[0, :, (4 * d + 2) * p:(4 * d + 3) * p] = si
        s_ref[0, :, (4 * d + 3) * p:(4 * d + 4) * p] = sr
        gr = gx_ref[0, d, 0]
        gi = gx_ref[0, d, 1]
        g_ref[0, (2 * d) * p:(2 * d + 1) * p, :] = (cre * gr - cim * gi).astype(BF16)
        g_ref[0, (2 * d + 1) * p:(2 * d + 2) * p, :] = (-(cre * gi + cim * gr)).astype(BF16)


def _s5_matrices(b_re, b_im, c_re, c_im, lam_re, lam_im, log_step):
    g, p, kk = b_re.shape
    lc = S5_CHUNK
    b_re, b_im, c_re, c_im = (a.astype(F32) for a in (b_re, b_im, c_re, c_im))
    lr = jnp.minimum(lam_re.astype(F32), -1e-4)
    li = lam_im.astype(F32)
    step = jnp.exp(log_step.astype(F32))[..., None]
    mag = jnp.exp(lr * step)
    ar, ai = mag * jnp.cos(li * step), mag * jnp.sin(li * step)
    den = lr * lr + li * li
    fr = ((ar - 1) * lr + ai * li) / den
    fi = (ai * lr - (ar - 1) * li) / den
    jj = jnp.arange(lc + 1, dtype=F32)[:, None, None, None]
    pmag = jnp.exp(jj * (lr * step)[None])
    pr = pmag * jnp.cos(jj * (li * step)[None])
    pi = pmag * jnp.sin(jj * (li * step)[None])
    wr = pr * fr[None] - pi * fi[None]
    wi = pr * fi[None] + pi * fr[None]

    def rep_cols(a):
        return jnp.repeat(jnp.transpose(a, (1, 2, 0)), kk, axis=-1)

    def rep_rows(a):
        return jnp.repeat(jnp.transpose(a, (1, 0, 2)), kk, axis=1)

    ctt_re = jnp.tile(jnp.transpose(c_re, (0, 2, 1)), (1, 1, lc))
    ctt_im = jnp.tile(jnp.transpose(c_im, (0, 2, 1)), (1, 1, lc))
    btt_re = jnp.tile(jnp.transpose(b_re, (0, 2, 1)), (1, lc, 1))
    btt_im = jnp.tile(jnp.transpose(b_im, (0, 2, 1)), (1, lc, 1))
    bt = jnp.concatenate([jnp.transpose(b_re, (0, 2, 1)), jnp.transpose(b_im, (0, 2, 1))], axis=-1)

    cw, vx, gx = [], [], []
    for d in range(2):
        lag = slice(0, lc)
        wjr, wji = wr[lag, d], wi[lag, d]
        wxr, wxi = rep_cols(wjr), rep_cols(wji)
        cwr = ctt_re * wxr - ctt_im * wxi
        cwi = ctt_re * wxi + ctt_im * wxr
        cw.append(jnp.concatenate([cwr, -cwi], axis=1))
        vr_s, vi_s = (wr[lag, d][::-1], wi[lag, d][::-1]) if d == 0 else (wr[lag, d], wi[lag, d])
        vx.append(jnp.stack([rep_rows(vr_s), rep_rows(vi_s)], axis=1))
        gr_t, gi_t = (pr[1:, d], pi[1:, d]) if d == 0 else (pr[1:, d][::-1], pi[1:, d][::-1])
        gx.append(jnp.stack([rep_cols(gr_t), rep_cols(gi_t)], axis=1))
    cw = jnp.stack(cw, axis=1)
    vx = jnp.stack(vx, axis=1)
    gx = jnp.stack(gx, axis=1)
    btt = jnp.stack([btt_re, btt_im], axis=1)
    ctt = jnp.stack([ctt_re, ctt_im], axis=1)

    blk = S5_BLK
    kall, smat, gcat = pl.pallas_call(
        _s5_prep_kernel,
        grid=(g,),
        in_specs=[pl.BlockSpec((1, kk, 2 * p), lambda i: (i, 0, 0)),
                  pl.BlockSpec((1, 2, 2 * p, blk), lambda i: (i, 0, 0, 0)),
                  pl.BlockSpec((1, 2, 2, blk, p), lambda i: (i, 0, 0, 0, 0)),
                  pl.BlockSpec((1, 2, blk, p), lambda i: (i, 0, 0, 0)),
                  pl.BlockSpec((1, 2, 2, p, blk), lambda i: (i, 0, 0, 0, 0)),
                  pl.BlockSpec((1, 2, p, blk), lambda i: (i, 0, 0, 0))],
        out_specs=[pl.BlockSpec((1, 2, kk, blk), lambda i: (i, 0, 0, 0)),
                   pl.BlockSpec((1, blk, 8 * p), lambda i: (i, 0, 0)),
                   pl.BlockSpec((1, 4 * p, blk), lambda i: (i, 0, 0))],
        out_shape=[jax.ShapeDtypeStruct((g, 2, kk, blk), F32),
                   jax.ShapeDtypeStruct((g, blk, 8 * p), BF16),
                   jax.ShapeDtypeStruct((g, 4 * p, blk), BF16)],
        compiler_params=_cparams(("arbitrary",), 24),
    )(bt, cw, vx, btt, gx, ctt)

    kf = kall[:, 0].reshape(g, kk, lc, kk)
    kb = kall[:, 1].reshape(g, kk, lc, kk)
    s_i = jnp.arange(lc)[:, None]
    t_i = jnp.arange(lc)[None, :]
    tf = jnp.where((t_i >= s_i)[None, None, :, :, None], kf[:, :, jnp.clip(t_i - s_i, 0, lc - 1), :], 0.0)
    tb = jnp.where((s_i >= t_i)[None, None, :, :, None], kb[:, :, jnp.clip(s_i - t_i, 0, lc - 1), :], 0.0)
    toep = jnp.transpose(tf + tb, (0, 2, 1, 3, 4)).reshape(g, blk, blk)
    rhs1 = jnp.concatenate([toep.astype(BF16), smat], axis=-1)

    a_r, a_i = pr[lc], pi[lc]
    atab = jnp.stack([jnp.concatenate([a_r, a_r], -1), jnp.concatenate([-a_i, a_i], -1),
                      jnp.concatenate([a_i, -a_i], -1)], axis=2)
    return rhs1, gcat, jnp.transpose(atab, (1, 0, 2, 3))


def _s5_kernel(u_ref, rhs_ref, g_ref, a_ref, o_ref, r_ref, *, gb, bsz, nch, ncc):
    p2 = 2 * S5_STATE
    for gi in range(gb):
        r_ref[gi] = _dot(u_ref[gi], rhs_ref[gi])
    a_tabs = [[[a_ref[gi, d, i:i + 1, :] for i in range(3)] for d in range(2)] for gi in range(gb)]

    def body(i, carry):
        cb = jnp.where(i < ncc, ncc - 1 - i, nch - 1 - (i - ncc))
        new = []
        for gi in range(gb):
            for d in range(2):
                h, hs = carry[2 * gi + d]
                c = i if d == 0 else cb
                rows = pl.ds(pl.multiple_of(c * bsz, bsz), bsz)
                base = S5_BLK + 2 * p2 * d
                s_in = r_ref[gi, rows, base:base + p2]
                s_sw = r_ref[gi, rows, base + p2:base + 2 * p2]
                r_ref[gi, rows, base:base + p2] = h
                a1, a2, a2s = a_tabs[gi][d]
                new.append((a1 * h + a2 * hs + s_in, a1 * hs + a2s * h + s_sw))
        return tuple(new)

    zero = jnp.zeros((bsz, p2), F32)
    lax.fori_loop(0, nch, body, tuple((zero, zero) for _ in range(2 * gb)))
    for gi in range(gb):
        hcat = jnp.concatenate([r_ref[gi, :, S5_BLK:S5_BLK + p2],
                                r_ref[gi, :, S5_BLK + 2 * p2:S5_BLK + 3 * p2]], axis=1).astype(BF16)
        o_ref[gi] = r_ref[gi, :, 0:S5_BLK] + _dot(hcat, g_ref[gi])


def _proj_tm_kernel(x_ref, w_ref, o_ref):
    acc = _dot(x_ref[0], w_ref[...])
    tm, n = acc.shape
    o_ref[:, 0] = acc.reshape(tm // S5_CHUNK, S5_CHUNK, n)


def _proj_tm(x, w):
    b, t, k = x.shape
    n = w.shape[1]
    tm = t // 2
    return pl.pallas_call(
        _proj_tm_kernel,
        grid=(b, t // tm),
        in_specs=[pl.BlockSpec((1, tm, k), lambda i, j: (i, j, 0)),
                  pl.BlockSpec((k, n), lambda i, j: (0, 0))],
        out_specs=pl.BlockSpec((tm // S5_CHUNK, 1, S5_CHUNK, n), lambda i, j: (j, i, 0, 0)),
        out_shape=jax.ShapeDtypeStruct((t // S5_CHUNK, b, S5_CHUNK, n), F32),
        compiler_params=_cparams(("arbitrary", "arbitrary"), 48),
    )(x, w)


def _lane_group(shape):
    return lax.broadcasted_iota(jnp.int32, shape, 1) // S5_GROUP


def _s5_pack_kernel(x_ref, o_ref, *, rt):
    sub = 16
    grp = _lane_group((sub, LANES))
    per = LANES // S5_GROUP
    for r0 in range(0, rt, sub):
        xs = [x_ref[pl.ds(r0 * S5_CHUNK + s, sub, stride=S5_CHUNK), :] for s in range(S5_CHUNK)]
        for g in range(per):
            for j in range(S5_CHUNK // per):
                col = jnp.zeros((sub, LANES), F32)
                for s8 in range(per):
                    shift = ((s8 - g) % per) * S5_GROUP
                    v = xs[per * j + s8]
                    v = pltpu.roll(v, shift, axis=1) if shift else v
                    col = jnp.where(grp == s8, v, col)
                o_ref[g, r0:r0 + sub, j * LANES:(j + 1) * LANES] = col.astype(o_ref.dtype)


def _s5_pack(u2, rows):
    w = u2.shape[1]
    per = LANES // S5_GROUP
    units = rows // 16
    rt = 16 * max(dv for dv in range(1, 18) if units % dv == 0)
    return pl.pallas_call(
        functools.partial(_s5_pack_kernel, rt=rt),
        grid=(w // LANES, rows // rt),
        in_specs=[pl.BlockSpec((rt * S5_CHUNK, LANES), lambda c, r: (r, c))],
        out_specs=pl.BlockSpec((per, rt, S5_BLK), lambda c, r: (c, r, 0)),
        out_shape=jax.ShapeDtypeStruct((S5_GROUPS, rows, S5_BLK), BF16),
        compiler_params=_cparams(("arbitrary", "arbitrary"), 32),
    )(u2)


def _s5_scan(ug, rhs1, gcat, atab, b, nch, n_ctx, layer):
    lc = S5_CHUNK
    g = S5_GROUPS
    rows = nch * b
    gb = 2
    first = layer * (g // gb)
    return pl.pallas_call(
        functools.partial(_s5_kernel, gb=gb, bsz=b, nch=nch, ncc=n_ctx // lc),
        grid=(g // gb,),
        in_specs=[pl.BlockSpec((gb, rows, S5_BLK), lambda i: (i, 0, 0)),
                  pl.BlockSpec((gb, S5_BLK, 3 * S5_BLK), lambda i: (first + i, 0, 0)),
                  pl.BlockSpec((gb, S5_BLK, S5_BLK), lambda i: (first + i, 0, 0)),
                  pl.BlockSpec((gb, 2, 3, 2 * S5_STATE), lambda i: (first + i, 0, 0, 0))],
        out_specs=pl.BlockSpec((gb, rows, S5_BLK), lambda i: (i, 0, 0)),
        out_shape=jax.ShapeDtypeStruct((g, rows, S5_BLK), F32),
        scratch_shapes=[pltpu.VMEM((gb, rows, 3 * S5_BLK), F32)],
        compiler_params=_cparams(("arbitrary",), 48),
    )(ug, rhs1, gcat, atab)


def _s5_glu_kernel(y_ref, u_ref, d_ref, w_ref, b_ref, o_ref, ybuf_ref, *, rt):
    sub = 8
    per = LANES // S5_GROUP
    grp = _lane_group((sub, LANES))
    for r0 in range(0, rt, sub):
        for cb in range(S5_GROUPS // per):
            ys = [[y_ref[cb * per + g, r0:r0 + sub, j * LANES:(j + 1) * LANES] for j in range(S5_CHUNK // per)]
                  for g in range(per)]
            for t in range(S5_CHUNK):
                col = jnp.zeros((sub, LANES), F32)
                for g in range(per):
                    shift = ((g - t % per) % per) * S5_GROUP
                    v = ys[g][t // per]
                    v = pltpu.roll(v, shift, axis=1) if shift else v
                    col = jnp.where(grp == g, v, col)
                ybuf_ref[cb, pl.ds(r0 * S5_CHUNK + t, sub, stride=S5_CHUNK), :] = col
    y = u_ref[...] * d_ref[...] + jnp.concatenate([ybuf_ref[cb] for cb in range(S5_GROUPS // per)], axis=1)
    y = 0.5 * y * (1.0 + jnp.tanh(math.sqrt(2.0 / math.pi) * (y + 0.044715 * (y * y * y))))
    gate = jax.nn.sigmoid(_dot(y.astype(BF16), w_ref[...]) + b_ref[...])
    o_ref[...] = (y * gate).astype(o_ref.dtype)


def _s5_glu(yg, u2, d_skip, w_glu, b_glu, b):
    g, rows, _ = yg.shape
    w = u2.shape[1]
    rt = 8 * b
    tok = pl.BlockSpec((rt * S5_CHUNK, w), lambda r: (r, 0))
    vec = pl.BlockSpec((1, w), lambda r: (0, 0))
    return pl.pallas_call(
        functools.partial(_s5_glu_kernel, rt=rt),
        grid=(rows // rt,),
        in_specs=[pl.BlockSpec((g, rt, S5_BLK), lambda r: (0, r, 0)), tok, vec,
                  pl.BlockSpec((w, w), lambda r: (0, 0)), vec],
        out_specs=tok,
        out_shape=jax.ShapeDtypeStruct(u2.shape, BF16),
        scratch_shapes=[pltpu.VMEM((w // LANES, rt * S5_CHUNK, LANES), F32)],
        compiler_params=_cparams(("arbitrary",), 40),
    )(yg, u2, d_skip.reshape(1, w).astype(F32), w_glu, b_glu.reshape(1, w).astype(F32))


def _branch_out_kernel(ya_ref, yb_ref, yc_ref, g_ref, x_ref, wa_ref, wb_ref, wc_ref, wo_ref,
                       ml_ref, mc_ref, nw_ref, xo_ref, ho_ref, *, tm, n_ctx):
    d = D_MODEL
    t = pl.program_id(1)
    g = g_ref[0]
    yb = yb_ref[:, 0].reshape(tm, yb_ref.shape[3])
    mix = (g[:, 0:d].astype(F32) * _dot(ya_ref[0], wa_ref[...])
           + g[:, d:2 * d].astype(F32) * _dot(yb, wb_ref[...])
           + g[:, 2 * d:3 * d].astype(F32) * _dot(yc_ref[0], wc_ref[...]))
    out = _dot(mix.astype(BF16), wo_ref[...])
    gate = _ctx_select(t, tm, n_ctx, mc_ref[2:3, :], ml_ref[0, 2:3, :])
    x1 = x_ref[0] + gate * out
    xo_ref[0] = x1
    shift = _ctx_select(t, tm, n_ctx, mc_ref[3:4, :], ml_ref[0, 3:4, :])
    scale = _ctx_select(t, tm, n_ctx, mc_ref[4:5, :], ml_ref[0, 4:5, :])
    ho_ref[0] = _rms_mod(x1, nw_ref[...], shift, scale).astype(ho_ref.dtype)


def _branch_out(ya, yb, yc, gates, x, wa, wb, wc, wo, m_lat, m_ctx, nw, n_ctx):
    b, t, d = x.shape
    tm = t // 8

    def row(n):
        return pl.BlockSpec((1, tm, n), lambda i, j: (i, j, 0))

    def full(a):
        return pl.BlockSpec(a.shape, lambda i, j: (0, 0))

    return pl.pallas_call(
        functools.partial(_branch_out_kernel, tm=tm, n_ctx=n_ctx),
        grid=(b, t // tm),
        in_specs=[row(ya.shape[2]),
                  pl.BlockSpec((tm // S5_CHUNK, 1, S5_CHUNK, yb.shape[3]), lambda i, j: (j, i, 0, 0)),
                  row(yc.shape[2]), row(gates.shape[2]), row(d),
                  full(wa), full(wb), full(wc), full(wo),
                  pl.BlockSpec((1, 6, d), lambda i, j: (i, 0, 0)),
                  pl.BlockSpec((6, d), lambda i, j: (0, 0)),
                  pl.BlockSpec((1, d), lambda i, j: (0, 0))],
        out_specs=[row(d), row(d)],
        out_shape=[jax.ShapeDtypeStruct((b, t, d), F32), jax.ShapeDtypeStruct((b, t, d), BF16)],
        compiler_params=_cparams(("arbitrary", "arbitrary"), 56),
    )(ya, yb, yc, gates, x, wa, wb, wc, wo, m_lat, m_ctx, nw.reshape(1, d))


def _ffn2_kernel(a_ref, x_ref, w_ref, ml_ref, mc_ref, nl_ref, nc_ref, nw_ref, xo_ref, ho_ref, *, tm, n_ctx):
    t = pl.program_id(1)
    gate = _ctx_select(t, tm, n_ctx, mc_ref[5:6, :], ml_ref[0, 5:6, :])
    x2 = x_ref[0] + gate * _dot(a_ref[0], w_ref[...])
    xo_ref[0] = x2
    shift = _ctx_select(t, tm, n_ctx, nc_ref[0:1, :], nl_ref[0, 0:1, :])
    scale = _ctx_select(t, tm, n_ctx, nc_ref[1:2, :], nl_ref[0, 1:2, :])
    ho_ref[0] = _rms_mod(x2, nw_ref[...], shift, scale).astype(ho_ref.dtype)


def _ffn2(a, x, w, m_lat, m_ctx, next_lat, next_ctx, nw, n_ctx, h_dtype):
    b, t, d = x.shape
    tm = t // 8
    kf = a.shape[2]

    def row(n):
        return pl.BlockSpec((1, tm, n), lambda i, j: (i, j, 0))

    lat = pl.BlockSpec((1, 6, d), lambda i, j: (i, 0, 0))
    ctx = pl.BlockSpec((6, d), lambda i, j: (0, 0))
    return pl.pallas_call(
        functools.partial(_ffn2_kernel, tm=tm, n_ctx=n_ctx),
        grid=(b, t // tm),
        in_specs=[row(kf), row(d), pl.BlockSpec((kf, d), lambda i, j: (0, 0)), lat, ctx, lat, ctx,
                  pl.BlockSpec((1, d), lambda i, j: (0, 0))],
        out_specs=[row(d), row(d)],
        out_shape=[jax.ShapeDtypeStruct((b, t, d), F32), jax.ShapeDtypeStruct((b, t, d), h_dtype)],
        compiler_params=_cparams(("arbitrary", "arbitrary"), 56),
    )(a, x, w, m_lat, m_ctx, next_lat, next_ctx, nw.reshape(1, d))


def _expanders():
    heads = jnp.arange(SSD_HEADS)
    e128 = (jnp.arange(SSD_HEADS * LANES)[None, :] // LANES == heads[:, None]).astype(BF16)
    return e128


def kernel(x, c, ctx, c_ctx, w_mod, b_mod, norm1_w, w_in, ssd_conv_w, ssd_conv_b, ssd_dt_bias, ssd_a_log, ssd_d, ssd_norm_w, s5_b_re, s5_b_im, s5_c_re, s5_c_im, s5_lam_re, s5_lam_im, s5_log_step, s5_d, s5_w_glu, s5_b_glu, da_lam_q1, da_lam_k1, da_lam_q2, da_lam_k2, da_subln_w, w_br_a, w_br_b, w_br_c, w_out, norm2_w, w_ff1, w_ff2, final_norm_w):
    bsz, n_lat, d = x.shape
    n_ctx = ctx.shape[1]
    depth = w_mod.shape[0]
    assert d == D_MODEL and n_ctx % SSD_CHUNK == 0 and n_lat % SSD_CHUNK == 0 and n_lat % GRID_W == 0
    t = n_ctx + n_lat
    assert t % 128 == 0

    rb = -(-(bsz + 1) // 8) * 8
    cvec = jnp.concatenate([c, c_ctx[None, :], jnp.zeros((rb - bsz - 1, d), F32)], axis=0)
    mods = _mods(cvec, w_mod, b_mod).reshape(depth, rb, 6, d)
    zeros_lat = jnp.zeros((bsz, 6, d), F32)
    zeros_ctx = jnp.zeros((6, d), F32)

    cos_t, sin_t = _rope_tables(n_ctx, n_lat)
    e128 = _expanders()
    q_scale = DA_HEAD_DIM ** -0.5 * LOG2E
    offs = [0]
    for s in IN_SPLITS:
        offs.append(offs[-1] + s)

    dg = depth * S5_GROUPS

    def per_group(a):
        return a.reshape((dg,) + a.shape[2:])

    def per_dir(a):
        return jnp.moveaxis(a, 1, 0).reshape((2, dg) + a.shape[3:])

    rhs1, gcat, atab = _s5_matrices(per_group(s5_b_re), per_group(s5_b_im), per_group(s5_c_re), per_group(s5_c_im),
                                    per_dir(s5_lam_re), per_dir(s5_lam_im), per_dir(s5_log_step))

    xs = jnp.concatenate([ctx, x], axis=1)
    h = _norm_mod(xs, mods[0, :bsz], mods[0, bsz], norm1_w[0], n_ctx)
    out = None
    for i in range(depth):
        m_lat, m_ctx = mods[i, :bsz], mods[i, bsz]
        wi = w_in[i].astype(BF16)
        w_z, w_xbc, w_dt, w_u, w_q, w_k, w_v, w_g = (wi[:, offs[j]:offs[j + 1]] for j in range(8))
        z = _proj(h, w_z, BF16)
        xbc = _proj(h, w_xbc, F32, tn=768)
        dt_raw = _proj(h, w_dt, F32)
        dt_raw_t = _proj_t(h, w_dt.T, F32)
        u2 = _proj_tm(h, w_u).reshape(t * bsz, S5_WIDTH)
        qk = _proj_rope(h, jnp.concatenate([w_q, w_k], axis=1), cos_t, sin_t, q_scale)
        vt = _proj_vt(h, w_v.T)
        gates = _proj(h, w_g, BF16, act="sigmoid")

        xact = _conv_silu(xbc, ssd_conv_w[i].astype(F32), ssd_conv_b[i].astype(F32), n_ctx)
        yf, yb = _ssd_scan(xact, dt_raw, dt_raw_t, ssd_dt_bias[i], ssd_a_log[i], e128, n_ctx)
        y_a = _ssd_final(xact, yf, yb, z, ssd_d[i], ssd_norm_w[i])

        nch = t // S5_CHUNK
        y_s5 = _s5_scan(_s5_pack(u2, nch * bsz), rhs1, gcat, atab, bsz, nch, n_ctx, i)
        y_b = _s5_glu(y_s5, u2, s5_d[i], s5_w_glu[i].astype(BF16), s5_b_glu[i], bsz)
        y_b = y_b.reshape(nch, bsz, S5_CHUNK, S5_WIDTH)

        lam_init = 0.8 - 0.6 * math.exp(-0.3 * i)
        lam_params = jnp.stack([da_lam_q1[i], da_lam_k1[i], da_lam_q2[i], da_lam_k2[i]], axis=0).astype(F32)
        y_c = _attention(qk, vt, lam_params, da_subln_w[i].astype(F32), n_ctx, lam_init)

        x1, h2 = _branch_out(y_a, y_b, y_c, gates, xs, w_br_a[i].astype(BF16), w_br_b[i].astype(BF16),
                             w_br_c[i].astype(BF16), w_out[i].astype(BF16), m_lat, m_ctx, norm2_w[i], n_ctx)
        act = _proj(h2, w_ff1[i].astype(BF16), BF16, act="relu2")
        if i + 1 < depth:
            xs, h = _ffn2(act, x1, w_ff2[i].astype(BF16), m_lat, m_ctx, mods[i + 1, :bsz], mods[i + 1, bsz],
                          norm1_w[i + 1], n_ctx, BF16)
        else:
            _, out = _ffn2(act, x1, w_ff2[i].astype(BF16), m_lat, m_ctx, zeros_lat, zeros_ctx,
                           final_norm_w, n_ctx, F32)
    return out[:, n_ctx:]
```

```python
import functools
import math

import jax
import jax.numpy as jnp
from jax import lax
from jax.experimental import pallas as pl
from jax.experimental.pallas import tpu as pltpu

F32 = jnp.float32
BF16 = jnp.bfloat16

D_MODEL = 1024
EPS = 1e-6
GRID_W = 64
SSD_HEAD_DIM = 64
SSD_INNER = D_MODEL
SSD_HEADS = SSD_INNER // SSD_HEAD_DIM
SSD_GROUPS = 4
SSD_STATE = 64
SSD_CONV = 5
SSD_CHUNK = 128
SSD_BC = SSD_GROUPS * SSD_STATE
SSD_CONV_DIM = SSD_INNER + 2 * SSD_BC
S5_WIDTH = 3 * D_MODEL // 4
S5_GROUP = 16
S5_GROUPS = S5_WIDTH // S5_GROUP
S5_STATE = 64
S5_CHUNK = 16
S5_BLK = S5_CHUNK * S5_GROUP
DA_HEAD_DIM = 64
DA_V_DIM = 2 * DA_HEAD_DIM
DA_HEADS = D_MODEL // DA_V_DIM
ROPE_BASE = 10000.0
D_FF = 4 * D_MODEL
N_BRANCH = 3
IN_SPLITS = (SSD_INNER, SSD_CONV_DIM, 2 * SSD_HEADS, S5_WIDTH, D_MODEL, D_MODEL, D_MODEL, N_BRANCH * D_MODEL)

LANES = 128
MIB = 1024 * 1024
LOG2E = 1.4426950408889634
VT_PAD = 16
ATTN_KEY_CHUNK = 256
ATTN_SPANS = 3

_HI = lax.Precision.HIGHEST


def _cparams(sem, vmem_mib):
    return pltpu.CompilerParams(dimension_semantics=sem, vmem_limit_bytes=int(vmem_mib * MIB))


def _dot(a, b):
    return jnp.dot(a, b, preferred_element_type=F32)


def _dot_nt(a, b):
    return lax.dot_general(a, b, (((1,), (1,)), ((), ())), preferred_element_type=F32)


def _ctx_select(t_idx, tm, n_ctx, ctx_row, lat_row):
    rows = t_idx * tm + lax.broadcasted_iota(jnp.int32, (tm, 1), 0)
    return jnp.where(rows < n_ctx, ctx_row, lat_row)


def _rms_mod(x, nw, shift, scale):
    y = x * lax.rsqrt(jnp.mean(x * x, axis=-1, keepdims=True) + EPS) * nw
    return y * (1.0 + scale) + shift


def _mods_kernel(s_ref, w_ref, b_ref, o_ref):
    s = s_ref[...]
    s = (s * jax.nn.sigmoid(s)).astype(BF16)
    o_ref[0] = _dot(s, w_ref[0]) + b_ref[0]


def _mods(cvec, w_mod, b_mod):
    depth, d, n = w_mod.shape
    rb = cvec.shape[0]
    tn = 1536
    return pl.pallas_call(
        _mods_kernel,
        grid=(depth, n // tn),
        in_specs=[pl.BlockSpec((rb, d), lambda i, j: (0, 0)),
                  pl.BlockSpec((1, d, tn), lambda i, j: (i, 0, j)),
                  pl.BlockSpec((1, 1, tn), lambda i, j: (i, 0, j))],
        out_specs=pl.BlockSpec((1, rb, tn), lambda i, j: (i, 0, j)),
        out_shape=jax.ShapeDtypeStruct((depth, rb, n), F32),
        compiler_params=_cparams(("arbitrary", "arbitrary"), 24),
    )(cvec, w_mod.astype(BF16), b_mod.reshape(depth, 1, n))


def _norm_mod_kernel(x_ref, ml_ref, mc_ref, nw_ref, o_ref, *, tm, n_ctx):
    t = pl.program_id(1)
    shift = _ctx_select(t, tm, n_ctx, mc_ref[0:1, :], ml_ref[0, 0:1, :])
    scale = _ctx_select(t, tm, n_ctx, mc_ref[1:2, :], ml_ref[0, 1:2, :])
    o_ref[0] = _rms_mod(x_ref[0], nw_ref[...], shift, scale).astype(o_ref.dtype)


def _norm_mod(x, m_lat, m_ctx, nw, n_ctx):
    b, t, d = x.shape
    tm = t // 8
    return pl.pallas_call(
        functools.partial(_norm_mod_kernel, tm=tm, n_ctx=n_ctx),
        grid=(b, t // tm),
        in_specs=[pl.BlockSpec((1, tm, d), lambda i, j: (i, j, 0)),
                  pl.BlockSpec((1, 6, d), lambda i, j: (i, 0, 0)),
                  pl.BlockSpec((6, d), lambda i, j: (0, 0)),
                  pl.BlockSpec((1, d), lambda i, j: (0, 0))],
        out_specs=pl.BlockSpec((1, tm, d), lambda i, j: (i, j, 0)),
        out_shape=jax.ShapeDtypeStruct((b, t, d), BF16),
        compiler_params=_cparams(("arbitrary", "arbitrary"), 32),
    )(x, m_lat, m_ctx, nw.reshape(1, d))


def _proj_kernel(x_ref, w_ref, o_ref, *, act):
    acc = _dot(x_ref[0], w_ref[0])
    if act == "sigmoid":
        acc = jax.nn.sigmoid(acc)
    elif act == "relu2":
        acc = jnp.maximum(acc, 0.0)
        acc = acc * acc
    o_ref[0] = acc.astype(o_ref.dtype)


def _proj(x, w, layer, out_dtype, act=None, tn=1024):
    b, t, k = x.shape
    n = w.shape[2]
    tn = min(tn, n)
    tm = t // 2
    return pl.pallas_call(
        functools.partial(_proj_kernel, act=act),
        grid=(b, t // tm, n // tn),
        in_specs=[pl.BlockSpec((1, tm, k), lambda i, j, l: (i, j, 0)),
                  pl.BlockSpec((1, k, tn), lambda i, j, l: (layer, 0, l))],
        out_specs=pl.BlockSpec((1, tm, tn), lambda i, j, l: (i, j, l)),
        out_shape=jax.ShapeDtypeStruct((b, t, n), out_dtype),
        compiler_params=_cparams(("arbitrary",) * 3, 56),
    )(x, w)


def _proj_t_kernel(x_ref, wt_ref, o_ref):
    o_ref[0] = _dot_nt(wt_ref[0], x_ref[0]).astype(o_ref.dtype)


def _proj_t(x, wt, layer, out_dtype, tn=256):
    b, t, k = x.shape
    n = wt.shape[1]
    tn = min(tn, n)
    tm = t // 2
    return pl.pallas_call(
        _proj_t_kernel,
        grid=(b, t // tm, n // tn),
        in_specs=[pl.BlockSpec((1, tm, k), lambda i, j, l: (i, j, 0)),
                  pl.BlockSpec((1, tn, k), lambda i, j, l: (layer, l, 0))],
        out_specs=pl.BlockSpec((1, tn, tm), lambda i, j, l: (i, l, j)),
        out_shape=jax.ShapeDtypeStruct((b, n, t), out_dtype),
        compiler_params=_cparams(("arbitrary",) * 3, 48),
    )(x, wt)


def _proj_rope_kernel(x_ref, w_ref, cos_ref, sin_ref, o_ref, *, tn, n_q_tiles, q_scale):
    acc = _dot(x_ref[0], w_ref[0])
    cos = cos_ref[...]
    sin = sin_ref[...]
    lane = lax.broadcasted_iota(jnp.int32, cos.shape, 1)
    first_half = (lane % 32) < 16
    scale = jnp.where(pl.program_id(2) < n_q_tiles, q_scale, 1.0).astype(F32)
    for c in range(tn // LANES):
        blk = acc[:, c * LANES:(c + 1) * LANES]
        partner = jnp.where(first_half, pltpu.roll(blk, LANES - 16, axis=1), pltpu.roll(blk, 16, axis=1))
        out = (blk * cos + partner * sin) * scale
        o_ref[0, :, c * LANES:(c + 1) * LANES] = out.astype(o_ref.dtype)


def _proj_rope(x, w, layer, cos, sin, q_scale):
    b, t, k = x.shape
    n = w.shape[2]
    tn = 512
    tm = t // 2
    return pl.pallas_call(
        functools.partial(_proj_rope_kernel, tn=tn, n_q_tiles=(n // 2) // tn, q_scale=q_scale),
        grid=(b, t // tm, n // tn),
        in_specs=[pl.BlockSpec((1, tm, k), lambda i, j, l: (i, j, 0)),
                  pl.BlockSpec((1, k, tn), lambda i, j, l: (layer, 0, l)),
                  pl.BlockSpec((tm, LANES), lambda i, j, l: (j, 0)),
                  pl.BlockSpec((tm, LANES), lambda i, j, l: (j, 0))],
        out_specs=pl.BlockSpec((1, tm, tn), lambda i, j, l: (i, j, l)),
        out_shape=jax.ShapeDtypeStruct((b, t, n), BF16),
        compiler_params=_cparams(("arbitrary",) * 3, 48),
    )(x, w, cos, sin)


def _rope_tables(n_ctx, n_lat):
    rows = n_lat // GRID_W
    row = jnp.repeat(jnp.arange(rows, dtype=F32), GRID_W)
    col = jnp.tile(jnp.arange(GRID_W, dtype=F32), rows)
    axis_dim = DA_HEAD_DIM // 2
    inv_freq = ROPE_BASE ** (-jnp.arange(0, axis_dim, 2, dtype=F32) / axis_dim)
    ang = jnp.stack([row[:, None] * inv_freq, col[:, None] * inv_freq], axis=1)
    cos = jnp.cos(ang)
    sin = jnp.sin(ang)
    cos_h = jnp.concatenate([cos, cos], axis=-1).reshape(n_lat, DA_HEAD_DIM)
    sin_h = jnp.concatenate([-sin, sin], axis=-1).reshape(n_lat, DA_HEAD_DIM)
    cos_t = jnp.concatenate([jnp.ones((n_ctx, DA_HEAD_DIM), F32), cos_h], axis=0)
    sin_t = jnp.concatenate([jnp.zeros((n_ctx, DA_HEAD_DIM), F32), sin_h], axis=0)
    return jnp.tile(cos_t, (1, 2)), jnp.tile(sin_t, (1, 2))


def _attn_kernel(q_ref, k_ref, vt_ref, lamp_ref, sw_ref, o_ref, *scratch, n_ctx, tq, kc, lam_init):
    s_refs = scratch
    qi = pl.program_id(2)
    lp = lamp_ref[...]
    lam = (jnp.exp(jnp.sum(lp[0:1] * lp[1:2], axis=-1, keepdims=True))
           - jnp.exp(jnp.sum(lp[2:3] * lp[3:4], axis=-1, keepdims=True)) + lam_init)
    q = q_ref[0]
    lane = lax.broadcasted_iota(jnp.int32, q.shape, 1)
    zero = jnp.zeros_like(q)
    q_both = jnp.concatenate([jnp.where(lane < DA_HEAD_DIM, q, zero), jnp.where(lane >= DA_HEAD_DIM, q, zero)],
                             axis=0)

    def attend(nk):
        n_slab = nk // kc
        n_span = min(ATTN_SPANS, n_slab)
        per = -(-n_slab // n_span) * kc
        spans = [(c0, min(c0 + per, nk)) for c0 in range(0, nk, per)]

        def scores(i):
            c0, c1 = spans[i]
            mx = None
            for r0 in range(c0, c1, kc):
                sv = _dot_nt(k_ref[0, r0:r0 + kc, :], q_both)
                s_refs[i][r0 - c0:r0 - c0 + kc, :] = sv
                sm = jnp.max(sv.reshape(kc // 8, 8, sv.shape[1]), axis=0)
                mx = sm if mx is None else jnp.maximum(mx, sm)
            return jnp.max(mx, axis=0, keepdims=True)

        parts = []
        mc_next = scores(0)
        for i, (c0, c1) in enumerate(spans):
            mc = mc_next
            if i + 1 < len(spans):
                mc_next = scores(i + 1)
            p = jnp.exp2(s_refs[i][0:c1 - c0, :] - mc).astype(BF16)
            parts.append((mc, _dot(vt_ref[0, 0, :, c0:c1], p)))
        m = functools.reduce(jnp.maximum, [mc for mc, _ in parts])
        acc = functools.reduce(lambda a, b: a + b, [a * jnp.exp2(mc - m) for mc, a in parts])
        acc = acc[0:DA_V_DIM] * (1.0 / acc[DA_V_DIM:DA_V_DIM + 1])
        o = acc[:, 0:tq] - lam * acc[:, tq:2 * tq]
        o = o * lax.rsqrt(jnp.mean(o * o, axis=0, keepdims=True) + EPS)
        o = o * (sw_ref[...] * (1.0 - lam_init))
        o_ref[0] = o.T.astype(o_ref.dtype)

    n_ctx_tiles = n_ctx // tq

    @pl.when(qi < n_ctx_tiles)
    def _():
        attend(n_ctx)

    @pl.when(qi >= n_ctx_tiles)
    def _():
        attend(k_ref.shape[1])


def _proj_vt_kernel(x_ref, wt_ref, o_ref, *, heads):
    acc = _dot_nt(wt_ref[0], x_ref[0])
    tm = acc.shape[1]
    pad = (lax.broadcasted_iota(jnp.int32, (VT_PAD, tm), 0) == 0).astype(o_ref.dtype)
    for h in range(heads):
        o_ref[0, h, 0:DA_V_DIM, :] = acc[h * DA_V_DIM:(h + 1) * DA_V_DIM].astype(o_ref.dtype)
        o_ref[0, h, DA_V_DIM:DA_V_DIM + VT_PAD, :] = pad


def _proj_vt(x, wt, layer):
    b, t, k = x.shape
    heads = 2
    tm = t // 2
    rows = DA_V_DIM + VT_PAD
    return pl.pallas_call(
        functools.partial(_proj_vt_kernel, heads=heads),
        grid=(b, t // tm, DA_HEADS // heads),
        in_specs=[pl.BlockSpec((1, tm, k), lambda i, j, l: (i, j, 0)),
                  pl.BlockSpec((1, heads * DA_V_DIM, k), lambda i, j, l: (layer, l, 0))],
        out_specs=pl.BlockSpec((1, heads, rows, tm), lambda i, j, l: (i, l, 0, j)),
        out_shape=jax.ShapeDtypeStruct((b, DA_HEADS, rows, t), BF16),
        compiler_params=_cparams(("arbitrary",) * 3, 48),
    )(x, wt)


def _attention(qk, vt, lam_params, subln_w, n_ctx, lam_init):
    b, t, _ = qk.shape
    tq = 256 if n_ctx % 256 == 0 else 128
    span = -(-(t // ATTN_KEY_CHUNK) // ATTN_SPANS) * ATTN_KEY_CHUNK
    return pl.pallas_call(
        functools.partial(_attn_kernel, n_ctx=n_ctx, tq=tq, kc=ATTN_KEY_CHUNK, lam_init=lam_init),
        grid=(b, DA_HEADS, t // tq),
        in_specs=[pl.BlockSpec((1, tq, LANES), lambda i, h, j: (i, j, h)),
                  pl.BlockSpec((1, t, LANES), lambda i, h, j: (i, 0, DA_HEADS + h)),
                  pl.BlockSpec((1, 1, DA_V_DIM + VT_PAD, t), lambda i, h, j: (i, h, 0, 0)),
                  pl.BlockSpec((4, DA_HEAD_DIM), lambda i, h, j: (0, 0)),
                  pl.BlockSpec((DA_V_DIM, 1), lambda i, h, j: (0, 0))],
        out_specs=pl.BlockSpec((1, tq, LANES), lambda i, h, j: (i, j, h)),
        out_shape=jax.ShapeDtypeStruct((b, t, D_MODEL), BF16),
        scratch_shapes=[pltpu.VMEM((span, 2 * tq), F32)] * ATTN_SPANS,
        compiler_params=_cparams(("arbitrary",) * 3, 56),
    )(qk, qk, vt, lam_params, subln_w.reshape(DA_V_DIM, 1))


def _conv_silu_kernel(x_ref, w_ref, b_ref, o_ref, *, n_ctx):
    x = x_ref[0]
    t = x.shape[0]
    rows = lax.broadcasted_iota(jnp.int32, (t, 1), 0)
    is_ctx = rows < n_ctx
    lo = jnp.where(is_ctx, 0, n_ctx)
    hi = jnp.where(is_ctx, n_ctx, t)
    acc = x * w_ref[SSD_CONV // 2:SSD_CONV // 2 + 1, :] + b_ref[...]
    for j in range(-(SSD_CONV // 2), SSD_CONV // 2 + 1):
        if j == 0:
            continue
        src = rows + j
        shifted = pltpu.roll(x, (-j) % t, axis=0)
        valid = (src >= lo) & (src < hi)
        acc = acc + jnp.where(valid, shifted, 0.0) * w_ref[j + SSD_CONV // 2:j + SSD_CONV // 2 + 1, :]
    o_ref[0] = acc * jax.nn.sigmoid(acc)


def _conv_silu(xbc, conv_w, conv_b, n_ctx):
    b, t, c = xbc.shape
    tc = 256
    return pl.pallas_call(
        functools.partial(_conv_silu_kernel, n_ctx=n_ctx),
        grid=(b, c // tc),
        in_specs=[pl.BlockSpec((1, t, tc), lambda i, j: (i, 0, j)),
                  pl.BlockSpec((SSD_CONV, tc), lambda i, j: (0, j)),
                  pl.BlockSpec((1, tc), lambda i, j: (0, j))],
        out_specs=pl.BlockSpec((1, t, tc), lambda i, j: (i, 0, j)),
        out_shape=jax.ShapeDtypeStruct((b, t, c), F32),
        compiler_params=_cparams(("arbitrary", "arbitrary"), 56),
    )(xbc, conv_w, conv_b.reshape(1, c))


def _softplus(x):
    return jnp.maximum(x, 0.0) + jnp.log1p(jnp.exp(-jnp.abs(x)))


def _expand(v, e_ref, pieces):
    e = e_ref[...]
    out = None
    rem = v
    for _ in range(pieces):
        part = rem.astype(BF16)
        term = _dot(part, e)
        out = term if out is None else out + term
        rem = rem - part.astype(F32)
    return out


def _ssd_prep(x_ref, dt_ref, dtt_ref, pr_ref, pc_ref, e128_ref, reverse, d):
    q = SSD_CHUNK
    nh = SSD_HEADS
    xa = x_ref[0]
    xh = xa[:, :SSD_INNER]
    bm = xa[:, SSD_INNER:SSD_INNER + SSD_BC]
    cm = xa[:, SSD_INNER + SSD_BC:]
    bias_r = pr_ref[0:1, :]
    a_r = -jnp.exp(pr_ref[1:2, :])
    bias_c = pc_ref[:, 0:1]
    a_c = -jnp.exp(pc_ref[:, 1:2])
    dt = _softplus(dt_ref[0][:, d * nh:(d + 1) * nh] + bias_r)
    dtt = _softplus(dtt_ref[0][d * nh:(d + 1) * nh, :] + bias_c)
    da = dt * a_r
    dat = dtt * a_c
    ri = lax.broadcasted_iota(jnp.int32, (q, q), 0)
    ci = lax.broadcasted_iota(jnp.int32, (q, q), 1)
    keep = (ci >= ri) if reverse else (ci <= ri)
    tri = keep.astype(F32)
    acum = jnp.dot(tri, da, precision=_HI, preferred_element_type=F32)
    acumt = lax.dot_general(dat, tri, (((1,), (1,)), ((), ())), precision=_HI,
                            preferred_element_type=F32)
    dt_w = _expand(dt, e128_ref, 2)
    ac_w = _expand(acum, e128_ref, 2)
    lane = lax.broadcasted_iota(jnp.int32, (q, LANES), 1)
    lo_half = lane < SSD_STATE
    zero_b = jnp.zeros((q, LANES), BF16)

    def head_pair(fn, i):
        return jnp.where(lo_half, fn(2 * i), fn(2 * i + 1))

    def wide(a, hh):
        return a[:, hh * LANES:(hh + 1) * LANES]

    def total_of(hh):
        return wide(ac_w, hh)[0:1, :] if reverse else wide(ac_w, hh)[q - 1:q, :]

    n_pair = nh // 2
    dt_x = jnp.concatenate([head_pair(lambda hh: wide(dt_w, hh), i) for i in range(n_pair)], axis=1)
    te_x = jnp.concatenate([head_pair(lambda hh: jnp.exp(total_of(hh) - wide(ac_w, hh)), i)
                            for i in range(n_pair)], axis=1)
    ea_x = jnp.concatenate([head_pair(lambda hh: jnp.exp(wide(ac_w, hh)), i) for i in range(n_pair)], axis=1)
    cd_x = jnp.concatenate([jnp.where(lo_half[0:1], jnp.exp(total_of(2 * i)), jnp.exp(total_of(2 * i + 1)))
                            for i in range(n_pair)], axis=1)
    ac_x = ac_w

    xs = xh * dt_x
    xs_b = xs.astype(BF16)
    xe_b = (xs * te_x).astype(BF16)
    return bm, cm, keep, acumt, ac_x, ea_x, cd_x, xs_b, xe_b


def _ssd_groups(prep, o_ref, h_ref):
    bm, cm, keep, acumt, ac_x, ea_x, cd_x, xs_b, xe_b = prep
    q = SSD_CHUNK
    lane = lax.broadcasted_iota(jnp.int32, (q, LANES), 1)
    lo_half = lane < SSD_STATE
    zero_b = jnp.zeros((q, LANES), BF16)
    y_cols = []
    for g in range(SSD_GROUPS):
        pair = g // 2
        bm_p = bm[:, pair * LANES:(pair + 1) * LANES]
        cm_p = cm[:, pair * LANES:(pair + 1) * LANES]
        mine = lo_half if g % 2 == 0 else jnp.logical_not(lo_half)
        cz = jnp.where(mine, cm_p, 0.0).astype(BF16)
        bt = bm_p.T.astype(BF16)
        cb = _dot(cz, bt)
        cols = slice(g * 4 * SSD_HEAD_DIM, (g + 1) * 4 * SSD_HEAD_DIM)
        hp = h_ref[g]
        y_off = _dot(cz, hp.astype(BF16)) * ea_x[:, cols]
        h_ref[g] = hp * cd_x[:, cols] + _dot(bt, xe_b[:, cols])
        y_parts = []
        for hpair in range(2):
            ws = []
            xin = []
            for sub in range(2):
                hh = g * 4 + hpair * 2 + sub
                seg = ac_x[:, hh * LANES:(hh + 1) * LANES] - acumt[hh:hh + 1, :]
                dec = jnp.exp(jnp.where(keep, seg, -jnp.inf))
                ws.append((cb * dec).astype(BF16))
                xp = xs_b[:, (hh // 2) * LANES:(hh // 2 + 1) * LANES]
                xin.append(jnp.where(lo_half if sub == 0 else jnp.logical_not(lo_half), xp, zero_b))
            y_parts.append(_dot(jnp.concatenate(ws, axis=1), jnp.concatenate(xin, axis=0)))
        y_cols.append(jnp.concatenate(y_parts, axis=1) + y_off)
    o_ref[0] = jnp.concatenate(y_cols, axis=1)


def _ssd_kernel(xf_ref, dtf_ref, dttf_ref, xb_ref, dtb_ref, dttb_ref, prf_ref, pcf_ref, prb_ref, pcb_ref,
                e128_ref, of_ref, ob_ref, h_ref):
    @pl.when(pl.program_id(1) == 0)
    def _():
        h_ref[...] = jnp.zeros_like(h_ref)

    prep_f = _ssd_prep(xf_ref, dtf_ref, dttf_ref, prf_ref, pcf_ref, e128_ref, False, 0)
    prep_b = _ssd_prep(xb_ref, dtb_ref, dttb_ref, prb_ref, pcb_ref, e128_ref, True, 1)
    _ssd_groups(prep_f, of_ref, h_ref.at[0])
    _ssd_groups(prep_b, ob_ref, h_ref.at[1])


def _ssd_scan(xact, dt_raw, dt_raw_t, dt_bias, a_log, e128, n_ctx):
    b, t, c = xact.shape
    q = SSD_CHUNK
    nc = t // q
    ncc = n_ctx // q

    def fwd(j):
        return j

    def bwd(j):
        return jnp.where(j < ncc, ncc - 1 - j, nc - 1 - (j - ncc))

    def specs(cidx):
        return [pl.BlockSpec((1, q, c), lambda i, j: (i, cidx(j), 0)),
                pl.BlockSpec((1, q, 2 * SSD_HEADS), lambda i, j: (i, cidx(j), 0)),
                pl.BlockSpec((1, 2 * SSD_HEADS, q), lambda i, j: (i, 0, cidx(j)))]

    prs = [jnp.stack([dt_bias[d], a_log[d]], axis=0).astype(F32) for d in range(2)]
    small = [pl.BlockSpec((2, SSD_HEADS), lambda i, j: (0, 0)), pl.BlockSpec((SSD_HEADS, 2), lambda i, j: (0, 0))]
    out_f = pl.BlockSpec((1, q, SSD_INNER), lambda i, j: (i, fwd(j), 0))
    out_b = pl.BlockSpec((1, q, SSD_INNER), lambda i, j: (i, bwd(j), 0))
    return pl.pallas_call(
        _ssd_kernel,
        grid=(b, nc),
        in_specs=specs(fwd) + specs(bwd) + small + small + [pl.BlockSpec(e128.shape, lambda i, j: (0, 0))],
        out_specs=[out_f, out_b],
        out_shape=[jax.ShapeDtypeStruct((b, t, SSD_INNER), F32)] * 2,
        scratch_shapes=[pltpu.VMEM((2, SSD_GROUPS, 2 * SSD_STATE, 4 * SSD_HEAD_DIM), F32)],
        compiler_params=_cparams(("arbitrary", "arbitrary"), 40),
    )(xact, dt_raw, dt_raw_t, xact, dt_raw, dt_raw_t, prs[0], prs[0].T, prs[1], prs[1].T, e128)


def _ssd_final_kernel(x_ref, yf_ref, yb_ref, z_ref, d_ref, nw_ref, o_ref):
    y = x_ref[0] * d_ref[...] + yf_ref[0] + yb_ref[0]
    z = z_ref[0].astype(F32)
    y = y * (z * jax.nn.sigmoid(z))
    gw = SSD_INNER // SSD_GROUPS
    for g in range(SSD_GROUPS):
        yg = y[:, g * gw:(g + 1) * gw]
        yg = yg * lax.rsqrt(jnp.mean(yg * yg, axis=-1, keepdims=True) + EPS)
        o_ref[0, :, g * gw:(g + 1) * gw] = (yg * nw_ref[:, g * gw:(g + 1) * gw]).astype(o_ref.dtype)


def _ssd_final(xact, yf, yb, z, d_skip, norm_w):
    b, t, _ = yf.shape
    tm = t // 8
    n = SSD_INNER
    row = pl.BlockSpec((1, tm, n), lambda i, j: (i, j, 0))
    vec = pl.BlockSpec((1, n), lambda i, j: (0, 0))
    return pl.pallas_call(
        _ssd_final_kernel,
        grid=(b, t // tm),
        in_specs=[row, row, row, row, vec, vec],
        out_specs=row,
        out_shape=jax.ShapeDtypeStruct((b, t, n), BF16),
        compiler_params=_cparams(("arbitrary", "arbitrary"), 48),
    )(xact, yf, yb, z, jnp.repeat(d_skip.astype(F32), SSD_HEAD_DIM).reshape(1, n), norm_w.reshape(1, n))


def _s5_prep_kernel(bt_ref, cw_ref, vx_ref, btt_ref, gx_ref, ctt_ref, kall_ref, s_ref, g_ref):
    for d in range(2):
        kall_ref[0, d] = jnp.dot(bt_ref[0], cw_ref[0, d], precision=_HI, preferred_element_type=F32)
    bre = btt_ref[0, 0]
    bim = btt_ref[0, 1]
    cre = ctt_ref[0, 0]
    cim = ctt_ref[0, 1]
    p = S5_STATE
    for d in range(2):
        vr = vx_ref[0, d, 0]
        vi = vx_ref[0, d, 1]
        sr = (vr * bre - vi * bim).astype(BF16)
        si = (vr * bim + vi * bre).astype(BF16)
        s_ref[0, :, (4 * d + 0) * p:(4 * d + 1) * p] = sr
        s_ref[0, :, (4 * d + 1) * p:(4 * d + 2) * p] = si
        s_ref[0, :, (4 * d + 2) * p:(4 * d + 3) * p] = si
        s_ref[0, :, (4 * d + 3) * p:(4 * d + 4) * p] = sr
        gr = gx_ref[0, d, 0]
        gi = gx_ref[0, d, 1]
        g_ref[0, (2 * d) * p:(2 * d + 1) * p, :] = (cre * gr - cim * gi).astype(BF16)
        g_ref[0, (2 * d + 1) * p:(2 * d + 2) * p, :] = (-(cre * gi + cim * gr)).astype(BF16)


def _s5_matrices(b_re, b_im, c_re, c_im, lam_re, lam_im, log_step):
    g, p, kk = b_re.shape
    lc = S5_CHUNK
    b_re, b_im, c_re, c_im = (a.astype(F32) for a in (b_re, b_im, c_re, c_im))
    lr = jnp.minimum(lam_re.astype(F32), -1e-4)
    li = lam_im.astype(F32)
    step = jnp.exp(log_step.astype(F32))[..., None]
    mag = jnp.exp(lr * step)
    ar, ai = mag * jnp.cos(li * step), mag * jnp.sin(li * step)
    den = lr * lr + li * li
    fr = ((ar - 1) * lr + ai * li) / den
    fi = (ai * lr - (ar - 1) * li) / den
    jj = jnp.arange(lc + 1, dtype=F32)[:, None, None, None]
    pmag = jnp.exp(jj * (lr * step)[None])
    pr = pmag * jnp.cos(jj * (li * step)[None])
    pi = pmag * jnp.sin(jj * (li * step)[None])
    wr = pr * fr[None] - pi * fi[None]
    wi = pr * fi[None] + pi * fr[None]

    def rep_cols(a):
        return jnp.repeat(jnp.transpose(a, (1, 2, 0)), kk, axis=-1)

    def rep_rows(a):
        return jnp.repeat(jnp.transpose(a, (1, 0, 2)), kk, axis=1)

    ctt_re = jnp.tile(jnp.transpose(c_re, (0, 2, 1)), (1, 1, lc))
    ctt_im = jnp.tile(jnp.transpose(c_im, (0, 2, 1)), (1, 1, lc))
    btt_re = jnp.tile(jnp.transpose(b_re, (0, 2, 1)), (1, lc, 1))
    btt_im = jnp.tile(jnp.transpose(b_im, (0, 2, 1)), (1, lc, 1))
    bt = jnp.concatenate([jnp.transpose(b_re, (0, 2, 1)), jnp.transpose(b_im, (0, 2, 1))], axis=-1)

    cw, vx, gx = [], [], []
    for d in range(2):
        lag = slice(0, lc)
        wjr, wji = wr[lag, d], wi[lag, d]
        wxr, wxi = rep_cols(wjr), rep_cols(wji)
        cwr = ctt_re * wxr - ctt_im * wxi
        cwi = ctt_re * wxi + ctt_im * wxr
        cw.append(jnp.concatenate([cwr, -cwi], axis=1))
        vr_s, vi_s = (wr[lag, d][::-1], wi[lag, d][::-1]) if d == 0 else (wr[lag, d], wi[lag, d])
        vx.append(jnp.stack([rep_rows(vr_s), rep_rows(vi_s)], axis=1))
        gr_t, gi_t = (pr[1:, d], pi[1:, d]) if d == 0 else (pr[1:, d][::-1], pi[1:, d][::-1])
        gx.append(jnp.stack([rep_cols(gr_t), rep_cols(gi_t)], axis=1))
    cw = jnp.stack(cw, axis=1)
    vx = jnp.stack(vx, axis=1)
    gx = jnp.stack(gx, axis=1)
    btt = jnp.stack([btt_re, btt_im], axis=1)
    ctt = jnp.stack([ctt_re, ctt_im], axis=1)

    blk = S5_BLK
    kall, smat, gcat = pl.pallas_call(
        _s5_prep_kernel,
        grid=(g,),
        in_specs=[pl.BlockSpec((1, kk, 2 * p), lambda i: (i, 0, 0)),
                  pl.BlockSpec((1, 2, 2 * p, blk), lambda i: (i, 0, 0, 0)),
                  pl.BlockSpec((1, 2, 2, blk, p), lambda i: (i, 0, 0, 0, 0)),
                  pl.BlockSpec((1, 2, blk, p), lambda i: (i, 0, 0, 0)),
                  pl.BlockSpec((1, 2, 2, p, blk), lambda i: (i, 0, 0, 0, 0)),
                  pl.BlockSpec((1, 2, p, blk), lambda i: (i, 0, 0, 0))],
        out_specs=[pl.BlockSpec((1, 2, kk, blk), lambda i: (i, 0, 0, 0)),
                   pl.BlockSpec((1, blk, 8 * p), lambda i: (i, 0, 0)),
                   pl.BlockSpec((1, 4 * p, blk), lambda i: (i, 0, 0))],
        out_shape=[jax.ShapeDtypeStruct((g, 2, kk, blk), F32),
                   jax.ShapeDtypeStruct((g, blk, 8 * p), BF16),
                   jax.ShapeDtypeStruct((g, 4 * p, blk), BF16)],
        compiler_params=_cparams(("arbitrary",), 24),
    )(bt, cw, vx, btt, gx, ctt)

    kf = kall[:, 0].reshape(g, kk, lc, kk)
    kb = kall[:, 1].reshape(g, kk, lc, kk)
    s_i = jnp.arange(lc)[:, None]
    t_i = jnp.arange(lc)[None, :]
    tf = jnp.where((t_i >= s_i)[None, None, :, :, None], kf[:, :, jnp.clip(t_i - s_i, 0, lc - 1), :], 0.0)
    tb = jnp.where((s_i >= t_i)[None, None, :, :, None], kb[:, :, jnp.clip(s_i - t_i, 0, lc - 1), :], 0.0)
    toep = jnp.transpose(tf + tb, (0, 2, 1, 3, 4)).reshape(g, blk, blk)
    rhs1 = jnp.concatenate([toep.astype(BF16), smat], axis=-1)

    a_r, a_i = pr[lc], pi[lc]
    atab = jnp.stack([jnp.concatenate([a_r, a_r], -1), jnp.concatenate([-a_i, a_i], -1),
                      jnp.concatenate([a_i, -a_i], -1)], axis=2)
    return rhs1, gcat, jnp.transpose(atab, (1, 0, 2, 3))


def _s5_kernel(u_ref, rhs_ref, g_ref, a_ref, o_ref, r_ref, *, gb, bsz, nch, ncc):
    p2 = 2 * S5_STATE
    for gi in range(gb):
        r_ref[gi] = _dot(u_ref[gi], rhs_ref[gi])
    a_tabs = [[[a_ref[gi, d, i:i + 1, :] for i in range(3)] for d in range(2)] for gi in range(gb)]

    def body(i, carry):
        cb = jnp.where(i < ncc, ncc - 1 - i, nch - 1 - (i - ncc))
        new = []
        for gi in range(gb):
            for d in range(2):
                h, hs = carry[2 * gi + d]
                c = i if d == 0 else cb
                rows = pl.ds(pl.multiple_of(c * bsz, bsz), bsz)
                base = S5_BLK + 2 * p2 * d
                s_in = r_ref[gi, rows, base:base + p2]
                s_sw = r_ref[gi, rows, base + p2:base + 2 * p2]
                r_ref[gi, rows, base:base + p2] = h
                a1, a2, a2s = a_tabs[gi][d]
                new.append((a1 * h + a2 * hs + s_in, a1 * hs + a2s * h + s_sw))
        return tuple(new)

    zero = jnp.zeros((bsz, p2), F32)
    lax.fori_loop(0, nch, body, tuple((zero, zero) for _ in range(2 * gb)))
    for gi in range(gb):
        hcat = jnp.concatenate([r_ref[gi, :, S5_BLK:S5_BLK + p2],
                                r_ref[gi, :, S5_BLK + 2 * p2:S5_BLK + 3 * p2]], axis=1).astype(BF16)
        o_ref[gi] = r_ref[gi, :, 0:S5_BLK] + _dot(hcat, g_ref[gi])


def _proj_tm_kernel(x_ref, w_ref, o_ref):
    acc = _dot(x_ref[0], w_ref[0])
    tm, n = acc.shape
    o_ref[:, 0] = acc.reshape(tm // S5_CHUNK, S5_CHUNK, n)


def _proj_tm(x, w, layer):
    b, t, k = x.shape
    n = w.shape[2]
    tm = t // 2
    return pl.pallas_call(
        _proj_tm_kernel,
        grid=(b, t // tm),
        in_specs=[pl.BlockSpec((1, tm, k), lambda i, j: (i, j, 0)),
                  pl.BlockSpec((1, k, n), lambda i, j: (layer, 0, 0))],
        out_specs=pl.BlockSpec((tm // S5_CHUNK, 1, S5_CHUNK, n), lambda i, j: (j, i, 0, 0)),
        out_shape=jax.ShapeDtypeStruct((t // S5_CHUNK, b, S5_CHUNK, n), F32),
        compiler_params=_cparams(("arbitrary", "arbitrary"), 48),
    )(x, w)


def _lane_group(shape):
    return lax.broadcasted_iota(jnp.int32, shape, 1) // S5_GROUP


def _s5_pack_kernel(x_ref, o_ref, *, rt):
    sub = 16
    grp = _lane_group((sub, LANES))
    per = LANES // S5_GROUP
    for r0 in range(0, rt, sub):
        xs = [x_ref[pl.ds(r0 * S5_CHUNK + s, sub, stride=S5_CHUNK), :] for s in range(S5_CHUNK)]
        for g in range(per):
            for j in range(S5_CHUNK // per):
                col = jnp.zeros((sub, LANES), F32)
                for s8 in range(per):
                    shift = ((s8 - g) % per) * S5_GROUP
                    v = xs[per * j + s8]
                    v = pltpu.roll(v, shift, axis=1) if shift else v
                    col = jnp.where(grp == s8, v, col)
                o_ref[g, r0:r0 + sub, j * LANES:(j + 1) * LANES] = col.astype(o_ref.dtype)


def _s5_pack(u2, rows):
    w = u2.shape[1]
    per = LANES // S5_GROUP
    units = rows // 16
    rt = 16 * max(dv for dv in range(1, 18) if units % dv == 0)
    return pl.pallas_call(
        functools.partial(_s5_pack_kernel, rt=rt),
        grid=(w // LANES, rows // rt),
        in_specs=[pl.BlockSpec((rt * S5_CHUNK, LANES), lambda c, r: (r, c))],
        out_specs=pl.BlockSpec((per, rt, S5_BLK), lambda c, r: (c, r, 0)),
        out_shape=jax.ShapeDtypeStruct((S5_GROUPS, rows, S5_BLK), BF16),
        compiler_params=_cparams(("arbitrary", "arbitrary"), 32),
    )(u2)


def _s5_scan(ug, rhs1, gcat, atab, b, nch, n_ctx, layer):
    lc = S5_CHUNK
    g = S5_GROUPS
    rows = nch * b
    gb = 2
    first = layer * (g // gb)
    return pl.pallas_call(
        functools.partial(_s5_kernel, gb=gb, bsz=b, nch=nch, ncc=n_ctx // lc),
        grid=(g // gb,),
        in_specs=[pl.BlockSpec((gb, rows, S5_BLK), lambda i: (i, 0, 0)),
                  pl.BlockSpec((gb, S5_BLK, 3 * S5_BLK), lambda i: (first + i, 0, 0)),
                  pl.BlockSpec((gb, S5_BLK, S5_BLK), lambda i: (first + i, 0, 0)),
                  pl.BlockSpec((gb, 2, 3, 2 * S5_STATE), lambda i: (first + i, 0, 0, 0))],
        out_specs=pl.BlockSpec((gb, rows, S5_BLK), lambda i: (i, 0, 0)),
        out_shape=jax.ShapeDtypeStruct((g, rows, S5_BLK), F32),
        scratch_shapes=[pltpu.VMEM((gb, rows, 3 * S5_BLK), F32)],
        compiler_params=_cparams(("arbitrary",), 48),
    )(ug, rhs1, gcat, atab)


def _s5_glu_kernel(y_ref, u_ref, d_ref, w_ref, b_ref, o_ref, ybuf_ref, *, rt):
    sub = 8
    per = LANES // S5_GROUP
    grp = _lane_group((sub, LANES))
    for r0 in range(0, rt, sub):
        for cb in range(S5_GROUPS // per):
            ys = [[y_ref[cb * per + g, r0:r0 + sub, j * LANES:(j + 1) * LANES] for j in range(S5_CHUNK // per)]
                  for g in range(per)]
            for t in range(S5_CHUNK):
                col = jnp.zeros((sub, LANES), F32)
                for g in range(per):
                    shift = ((g - t % per) % per) * S5_GROUP
                    v = ys[g][t // per]
                    v = pltpu.roll(v, shift, axis=1) if shift else v
                    col = jnp.where(grp == g, v, col)
                ybuf_ref[cb, pl.ds(r0 * S5_CHUNK + t, sub, stride=S5_CHUNK), :] = col
    y = u_ref[...] * d_ref[...] + jnp.concatenate([ybuf_ref[cb] for cb in range(S5_GROUPS // per)], axis=1)
    y = 0.5 * y * (1.0 + jnp.tanh(math.sqrt(2.0 / math.pi) * (y + 0.044715 * (y * y * y))))
    gate = jax.nn.sigmoid(_dot(y.astype(BF16), w_ref[0]) + b_ref[...])
    o_ref[...] = (y * gate).astype(o_ref.dtype)


def _s5_glu(yg, u2, d_skip, w_glu, layer, b_glu, b):
    g, rows, _ = yg.shape
    w = u2.shape[1]
    rt = 8 * b
    tok = pl.BlockSpec((rt * S5_CHUNK, w), lambda r: (r, 0))
    vec = pl.BlockSpec((1, w), lambda r: (0, 0))
    return pl.pallas_call(
        functools.partial(_s5_glu_kernel, rt=rt),
        grid=(rows // rt,),
        in_specs=[pl.BlockSpec((g, rt, S5_BLK), lambda r: (0, r, 0)), tok, vec,
                  pl.BlockSpec((1, w, w), lambda r: (layer, 0, 0)), vec],
        out_specs=tok,
        out_shape=jax.ShapeDtypeStruct(u2.shape, BF16),
        scratch_shapes=[pltpu.VMEM((w // LANES, rt * S5_CHUNK, LANES), F32)],
        compiler_params=_cparams(("arbitrary",), 40),
    )(yg, u2, d_skip.reshape(1, w).astype(F32), w_glu, b_glu.reshape(1, w).astype(F32))


def _branch_out_kernel(ya_ref, yb_ref, yc_ref, g_ref, x_ref, wa_ref, wb_ref, wc_ref, wo_ref,
                       ml_ref, mc_ref, nw_ref, xo_ref, ho_ref, *, tm, n_ctx):
    d = D_MODEL
    t = pl.program_id(1)
    g = g_ref[0]
    yb = yb_ref[:, 0].reshape(tm, yb_ref.shape[3])
    mix = (g[:, 0:d].astype(F32) * _dot(ya_ref[0], wa_ref[0])
           + g[:, d:2 * d].astype(F32) * _dot(yb, wb_ref[0])
           + g[:, 2 * d:3 * d].astype(F32) * _dot(yc_ref[0], wc_ref[0]))
    out = _dot(mix.astype(BF16), wo_ref[0])
    gate = _ctx_select(t, tm, n_ctx, mc_ref[2:3, :], ml_ref[0, 2:3, :])
    x1 = x_ref[0] + gate * out
    xo_ref[0] = x1
    shift = _ctx_select(t, tm, n_ctx, mc_ref[3:4, :], ml_ref[0, 3:4, :])
    scale = _ctx_select(t, tm, n_ctx, mc_ref[4:5, :], ml_ref[0, 4:5, :])
    ho_ref[0] = _rms_mod(x1, nw_ref[...], shift, scale).astype(ho_ref.dtype)


def _branch_out(ya, yb, yc, gates, x, wa, wb, wc, wo, layer, m_lat, m_ctx, nw, n_ctx):
    b, t, d = x.shape
    tm = t // 8

    def row(n):
        return pl.BlockSpec((1, tm, n), lambda i, j: (i, j, 0))

    def full(a):
        return pl.BlockSpec((1,) + a.shape[1:], lambda i, j: (layer, 0, 0))

    return pl.pallas_call(
        functools.partial(_branch_out_kernel, tm=tm, n_ctx=n_ctx),
        grid=(b, t // tm),
        in_specs=[row(ya.shape[2]),
                  pl.BlockSpec((tm // S5_CHUNK, 1, S5_CHUNK, yb.shape[3]), lambda i, j: (j, i, 0, 0)),
                  row(yc.shape[2]), row(gates.shape[2]), row(d),
                  full(wa), full(wb), full(wc), full(wo),
                  pl.BlockSpec((1, 6, d), lambda i, j: (i, 0, 0)),
                  pl.BlockSpec((6, d), lambda i, j: (0, 0)),
                  pl.BlockSpec((1, d), lambda i, j: (0, 0))],
        out_specs=[row(d), row(d)],
        out_shape=[jax.ShapeDtypeStruct((b, t, d), F32), jax.ShapeDtypeStruct((b, t, d), BF16)],
        compiler_params=_cparams(("arbitrary", "arbitrary"), 56),
    )(ya, yb, yc, gates, x, wa, wb, wc, wo, m_lat, m_ctx, nw.reshape(1, d))


def _ffn2_kernel(a_ref, x_ref, w_ref, ml_ref, mc_ref, nl_ref, nc_ref, nw_ref, xo_ref, ho_ref, *, tm, n_ctx):
    t = pl.program_id(1)
    gate = _ctx_select(t, tm, n_ctx, mc_ref[5:6, :], ml_ref[0, 5:6, :])
    x2 = x_ref[0] + gate * _dot(a_ref[0], w_ref[0])
    xo_ref[0] = x2
    shift = _ctx_select(t, tm, n_ctx, nc_ref[0:1, :], nl_ref[0, 0:1, :])
    scale = _ctx_select(t, tm, n_ctx, nc_ref[1:2, :], nl_ref[0, 1:2, :])
    ho_ref[0] = _rms_mod(x2, nw_ref[...], shift, scale).astype(ho_ref.dtype)


def _ffn2(a, x, w, layer, m_lat, m_ctx, next_lat, next_ctx, nw, n_ctx, h_dtype):
    b, t, d = x.shape
    tm = t // 8
    kf = a.shape[2]

    def row(n):
        return pl.BlockSpec((1, tm, n), lambda i, j: (i, j, 0))

    lat = pl.BlockSpec((1, 6, d), lambda i, j: (i, 0, 0))
    ctx = pl.BlockSpec((6, d), lambda i, j: (0, 0))
    return pl.pallas_call(
        functools.partial(_ffn2_kernel, tm=tm, n_ctx=n_ctx),
        grid=(b, t // tm),
        in_specs=[row(kf), row(d), pl.BlockSpec((1, kf, d), lambda i, j: (layer, 0, 0)), lat, ctx, lat, ctx,
                  pl.BlockSpec((1, d), lambda i, j: (0, 0))],
        out_specs=[row(d), row(d)],
        out_shape=[jax.ShapeDtypeStruct((b, t, d), F32), jax.ShapeDtypeStruct((b, t, d), h_dtype)],
        compiler_params=_cparams(("arbitrary", "arbitrary"), 56),
    )(a, x, w, m_lat, m_ctx, next_lat, next_ctx, nw.reshape(1, d))


def _expanders():
    heads = jnp.arange(SSD_HEADS)
    e128 = (jnp.arange(SSD_HEADS * LANES)[None, :] // LANES == heads[:, None]).astype(BF16)
    return e128


def kernel(x, c, ctx, c_ctx, w_mod, b_mod, norm1_w, w_in, ssd_conv_w, ssd_conv_b, ssd_dt_bias, ssd_a_log, ssd_d, ssd_norm_w, s5_b_re, s5_b_im, s5_c_re, s5_c_im, s5_lam_re, s5_lam_im, s5_log_step, s5_d, s5_w_glu, s5_b_glu, da_lam_q1, da_lam_k1, da_lam_q2, da_lam_k2, da_subln_w, w_br_a, w_br_b, w_br_c, w_out, norm2_w, w_ff1, w_ff2, final_norm_w):
    bsz, n_lat, d = x.shape
    n_ctx = ctx.shape[1]
    depth = w_mod.shape[0]
    assert d == D_MODEL and n_ctx % SSD_CHUNK == 0 and n_lat % SSD_CHUNK == 0 and n_lat % GRID_W == 0
    t = n_ctx + n_lat
    assert t % 128 == 0

    rb = -(-(bsz + 1) // 8) * 8
    cvec = jnp.concatenate([c, c_ctx[None, :], jnp.zeros((rb - bsz - 1, d), F32)], axis=0)
    mods = _mods(cvec, w_mod, b_mod).reshape(depth, rb, 6, d)
    zeros_lat = jnp.zeros((bsz, 6, d), F32)
    zeros_ctx = jnp.zeros((6, d), F32)

    cos_t, sin_t = _rope_tables(n_ctx, n_lat)
    e128 = _expanders()
    q_scale = DA_HEAD_DIM ** -0.5 * LOG2E
    offs = [0]
    for s in IN_SPLITS:
        offs.append(offs[-1] + s)

    dg = depth * S5_GROUPS

    def per_group(a):
        return a.reshape((dg,) + a.shape[2:])

    def per_dir(a):
        return jnp.moveaxis(a, 1, 0).reshape((2, dg) + a.shape[3:])

    rhs1, gcat, atab = _s5_matrices(per_group(s5_b_re), per_group(s5_b_im), per_group(s5_c_re), per_group(s5_c_im),
                                    per_dir(s5_lam_re), per_dir(s5_lam_im), per_dir(s5_log_step))

    wib = w_in.astype(BF16)
    w_z, w_xbc, w_dt, w_u, w_q, w_k, w_v, w_g = (wib[:, :, offs[j]:offs[j + 1]] for j in range(8))
    w_qk = jnp.concatenate([w_q, w_k], axis=2)
    w_dt_t = jnp.swapaxes(w_dt, 1, 2)
    w_v_t = jnp.swapaxes(w_v, 1, 2)
    wa, wb, wc, wo = (a.astype(BF16) for a in (w_br_a, w_br_b, w_br_c, w_out))
    wf1, wf2, w_glu = (a.astype(BF16) for a in (w_ff1, w_ff2, s5_w_glu))

    xs = jnp.concatenate([ctx, x], axis=1)
    h = _norm_mod(xs, mods[0, :bsz], mods[0, bsz], norm1_w[0], n_ctx)
    out = None
    for i in range(depth):
        m_lat, m_ctx = mods[i, :bsz], mods[i, bsz]
        z = _proj(h, w_z, i, BF16)
        xbc = _proj(h, w_xbc, i, F32, tn=768)
        dt_raw = _proj(h, w_dt, i, F32)
        dt_raw_t = _proj_t(h, w_dt_t, i, F32)
        u2 = _proj_tm(h, w_u, i).reshape(t * bsz, S5_WIDTH)
        qk = _proj_rope(h, w_qk, i, cos_t, sin_t, q_scale)
        vt = _proj_vt(h, w_v_t, i)
        gates = _proj(h, w_g, i, BF16, act="sigmoid")

        xact = _conv_silu(xbc, ssd_conv_w[i].astype(F32), ssd_conv_b[i].astype(F32), n_ctx)
        yf, yb = _ssd_scan(xact, dt_raw, dt_raw_t, ssd_dt_bias[i], ssd_a_log[i], e128, n_ctx)
        y_a = _ssd_final(xact, yf, yb, z, ssd_d[i], ssd_norm_w[i])

        nch = t // S5_CHUNK
        y_s5 = _s5_scan(_s5_pack(u2, nch * bsz), rhs1, gcat, atab, bsz, nch, n_ctx, i)
        y_b = _s5_glu(y_s5, u2, s5_d[i], w_glu, i, s5_b_glu[i], bsz)
        y_b = y_b.reshape(nch, bsz, S5_CHUNK, S5_WIDTH)

        lam_init = 0.8 - 0.6 * math.exp(-0.3 * i)
        lam_params = jnp.stack([da_lam_q1[i], da_lam_k1[i], da_lam_q2[i], da_lam_k2[i]], axis=0).astype(F32)
        y_c = _attention(qk, vt, lam_params, da_subln_w[i].astype(F32), n_ctx, lam_init)

        x1, h2 = _branch_out(y_a, y_b, y_c, gates, xs, wa, wb, wc, wo, i, m_lat, m_ctx, norm2_w[i], n_ctx)
        act = _proj(h2, wf1, i, BF16, act="relu2")
        if i + 1 < depth:
            xs, h = _ffn2(act, x1, wf2, i, m_lat, m_ctx, mods[i + 1, :bsz], mods[i + 1, bsz],
                          norm1_w[i + 1], n_ctx, BF16)
        else:
            _, out = _ffn2(act, x1, wf2, i, m_lat, m_ctx, zeros_lat, zeros_ctx, final_norm_w, n_ctx, F32)
    return out[:, n_ctx:]
```
